```python
import math
import jax
import jax.numpy as jnp
from jax import lax
import numpy as np

D_MODEL = 2048
BATCH = 16
SEQ = 2048
DEPTH = 4

CTX_LEN = 256
GRID_W = 64
N_EVEN = (DEPTH + 1) // 2
N_ODD = DEPTH // 2
HALF_MIX = D_MODEL // 2
MIX_WIDTH = 2 * HALF_MIX

S5_CH = HALF_MIX
S5_GROUP_CH = 16
S5_GROUPS = S5_CH // S5_GROUP_CH
S5_STATE = 64
S5_DT_MIN = 1e-3
S5_DT_MAX = 1e-1

NA_HEAD_DIM = 128
NA_HEADS = HALF_MIX // NA_HEAD_DIM
NA_ROWS = 8
NA_COLS = 16
NA_QCOLS = 16
NA_KCOLS = NA_QCOLS + NA_COLS

DIFF_HEAD_DIM = 128
DIFF_HEADS = HALF_MIX // (2 * DIFF_HEAD_DIM)

SWA_HEAD_DIM = 128
SWA_HEADS = HALF_MIX // SWA_HEAD_DIM
SWA_KV_HEADS = SWA_HEADS // 4
SWA_WINDOW = 128
SWA_BLOCK = 128
SWA_SPAN = SWA_BLOCK + 2 * SWA_WINDOW
Q_BLOCK = 128

FFN_HIDDEN = ((8 * D_MODEL // 3 + 255) // 256) * 256
CONV_W = 3

ROPE_BASE = 10000.0
ROPE_AXIS_DIM = 64
ROPE_FREQS = ROPE_AXIS_DIM // 2

NORM_EPS = 1e-6
NEG_INF = -1e30
F32 = jnp.float32

EV_IN = S5_CH + 3 * HALF_MIX
OD_SIZES = (HALF_MIX, HALF_MIX, HALF_MIX, HALF_MIX,
            SWA_KV_HEADS * SWA_HEAD_DIM, SWA_KV_HEADS * SWA_HEAD_DIM)
OD_IN = sum(OD_SIZES)
OD_CUTS = tuple(int(v) for v in np.cumsum(OD_SIZES)[:-1])

kernel_name = 'hybrid_s5_natten_diffattn_swa_dit'


def rms_norm(x, g):
    xf = x.astype(F32)
    y = xf * lax.rsqrt(jnp.mean(xf * xf, axis=-1, keepdims=True) + NORM_EPS)
    return (y * g.astype(F32)).astype(x.dtype)


def rope_2d_tables(L):
    t = jnp.arange(L, dtype=jnp.int32)
    pos = jnp.stack([t // GRID_W, t % GRID_W], axis=-1).astype(F32)
    inv = ROPE_BASE ** (-2.0 * jnp.arange(ROPE_FREQS, dtype=F32) / ROPE_AXIS_DIM)
    ang = pos[:, :, None] * inv
    return jnp.cos(ang), jnp.sin(ang)


def apply_rope_2d(x, cos, sin):
    xs = x.astype(F32).reshape(x.shape[:-1] + (2, 2, ROPE_FREQS))
    x1, x2 = xs[..., 0, :], xs[..., 1, :]
    c, s = cos[:, None], sin[:, None]
    out = jnp.stack([x1 * c - x2 * s, x2 * c + x1 * s], axis=-2)
    return out.reshape(x.shape).astype(x.dtype)


def ctx_self_attention(q, k, v, sink=None):
    B_, C, H, dh = q.shape
    KVH = k.shape[2]
    qg = q.reshape(B_, C, KVH, H // KVH, dh)
    s = jnp.einsum('bqkgd,bskd->bkgqs', qg, k).astype(F32) * dh ** -0.5
    if sink is not None:
        sk = jnp.broadcast_to(sink.astype(F32).reshape(1, KVH, H // KVH, 1, 1), s.shape[:-1] + (1,))
        s = jnp.concatenate([s, sk], axis=-1)
    p = jax.nn.softmax(s, axis=-1)[..., :C].astype(v.dtype)
    o = jnp.einsum('bkgqs,bskd->bqkgd', p, v)
    return o.reshape(B_, C, H * dh)


def s5_discretize(lam_re, lam_im, log_dt, b_re, b_im):
    lam_re = jnp.minimum(lam_re.astype(F32), -1e-4)
    lam_im = lam_im.astype(F32)
    dt = jnp.exp(log_dt.astype(F32))[:, None]
    mag = jnp.exp(lam_re * dt)
    lb_re, lb_im = mag * jnp.cos(lam_im * dt), mag * jnp.sin(lam_im * dt)
    den = lam_re * lam_re + lam_im * lam_im
    num_re = lb_re - 1.0
    coef_re = (num_re * lam_re + lb_im * lam_im) / den
    coef_im = (lb_im * lam_re - num_re * lam_im) / den
    b_re, b_im = b_re.astype(F32), b_im.astype(F32)
    bb_re = coef_re[..., None] * b_re - coef_im[..., None] * b_im
    bb_im = coef_re[..., None] * b_im + coef_im[..., None] * b_re
    return lb_re, lb_im, bb_re, bb_im


def _complex_affine_combine(e1, e2):
    a1r, a1i, b1r, b1i = e1
    a2r, a2i, b2r, b2i = e2
    return (a2r * a1r - a2i * a1i, a2r * a1i + a2i * a1r,
            a2r * b1r - a2i * b1i + b2r, a2r * b1i + a2i * b1r + b2i)


def diag_scan(lb_re, lb_im, bu_re, bu_im, h0=None):
    if h0 is not None:
        bu_re = bu_re.at[0].add(lb_re * h0[0] - lb_im * h0[1])
        bu_im = bu_im.at[0].add(lb_re * h0[1] + lb_im * h0[0])
    shape = (bu_re.shape[0], 1) + lb_re.shape
    elems = (jnp.broadcast_to(lb_re, shape), jnp.broadcast_to(lb_im, shape), bu_re, bu_im)
    _, _, h_re, h_im = lax.associative_scan(_complex_affine_combine, elems, axis=0)
    return h_re, h_im


def s5_direction(u_ctx, u_lat, lb_re, lb_im, bb_re, bb_im, reverse):
    def drive(t):
        pair = (jnp.einsum('tbgh,gph->tbgp', t, bb_re), jnp.einsum('tbgh,gph->tbgp', t, bb_im))
        return (jnp.flip(pair[0], 0), jnp.flip(pair[1], 0)) if reverse else pair
    hc = diag_scan(lb_re, lb_im, *drive(u_ctx))
    hx = diag_scan(lb_re, lb_im, *drive(u_lat), h0=(hc[0][-1], hc[1][-1]))
    if reverse:
        hc = (jnp.flip(hc[0], 0), jnp.flip(hc[1], 0))
        hx = (jnp.flip(hx[0], 0), jnp.flip(hx[1], 0))
    return hc, hx


def s5_readout(h, c_re, c_im):
    return (jnp.einsum('tbgp,ghp->tbgh', h[0], c_re.astype(F32))
            - jnp.einsum('tbgp,ghp->tbgh', h[1], c_im.astype(F32)))


def s5_mixer(u, uc, lam_re, lam_im, log_dt, b_re, b_im, c_re, c_im, d_skip, w_glu, with_ctx):
    def time_major(t):
        return t.astype(F32).reshape(t.shape[0], t.shape[1], S5_GROUPS, S5_GROUP_CH).transpose(1, 0, 2, 3)
    ut, uct = time_major(u), time_major(uc)
    d = d_skip.astype(F32).reshape(S5_GROUPS, S5_GROUP_CH)
    y = d * ut
    yc = d * uct if with_ctx else None
    for r in range(2):
        lb_re, lb_im, bb_re, bb_im = s5_discretize(lam_re[r], lam_im[r], log_dt[r], b_re[r], b_im[r])
        hc, hx = s5_direction(uct, ut, lb_re, lb_im, bb_re, bb_im, reverse=(r == 1))
        y = y + s5_readout(hx, c_re[r], c_im[r])
        if with_ctx:
            yc = yc + s5_readout(hc, c_re[r], c_im[r])

    def glu(t):
        t = jax.nn.gelu(t.transpose(1, 0, 2, 3).reshape(t.shape[1], t.shape[0], S5_CH)).astype(u.dtype)
        return t * jax.nn.sigmoid(t @ w_glu)

    return glu(y), (glu(yc) if with_ctx else None)


def neighborhood_attention(q, k, v, kc, vc, rpb):
    B_, L, H, dh = q.shape
    rows = L // GRID_W
    kr = min(NA_ROWS, rows)
    scale = dh ** -0.5
    nqb = GRID_W // NA_QCOLS
    qg = q.reshape(B_, rows, GRID_W, H, dh)
    kg = k.reshape(B_, rows, GRID_W, H, dh)
    vg = v.reshape(B_, rows, GRID_W, H, dh)
    q_cols = jnp.arange(GRID_W).reshape(nqb, NA_QCOLS)
    win_start = jnp.clip(q_cols - NA_COLS // 2, 0, GRID_W - NA_COLS)
    k_cols = (jnp.clip(q_cols[:, :1] - NA_COLS // 2, 0, GRID_W - NA_KCOLS)
              + jnp.arange(NA_KCOLS))
    col_valid = ((k_cols[:, None, :] >= win_start[..., None])
                 & (k_cols[:, None, :] < win_start[..., None] + NA_COLS))
    col_idx = jnp.clip(k_cols[:, None, :] - q_cols[..., None] + NA_COLS - 1, 0, 2 * NA_COLS - 2)
    rpb_f = rpb.astype(F32)
    n_loc = kr * NA_KCOLS

    def row_step(r):
        rs = jnp.clip(r - kr // 2, 0, rows - kr)
        q_r = lax.dynamic_index_in_dim(qg, r, axis=1, keepdims=False).reshape(B_, nqb, NA_QCOLS, H, dh)
        k_r = lax.dynamic_slice_in_dim(kg, rs, kr, axis=1)[:, :, k_cols]
        v_r = lax.dynamic_slice_in_dim(vg, rs, kr, axis=1)[:, :, k_cols]
        row_idx = rs + jnp.arange(kr) - r + NA_ROWS - 1
        bias = rpb_f[:, row_idx][:, :, col_idx]
        bias = jnp.where(col_valid, bias, NEG_INF).transpose(0, 2, 3, 1, 4)
        s_loc = jnp.einsum('bnqhd,brnkhd->bhnqrk', q_r, k_r).astype(F32) * scale + bias
        s_ctx = jnp.einsum('bnqhd,bchd->bhnqc', q_r, kc).astype(F32) * scale
        logits = jnp.concatenate([s_loc.reshape(B_, H, nqb, NA_QCOLS, n_loc), s_ctx], axis=-1)
        p = jax.nn.softmax(logits, axis=-1).astype(v.dtype)
        o = (jnp.einsum('bhnqrk,brnkhd->bnqhd', p[..., :n_loc].reshape(s_loc.shape), v_r)
             + jnp.einsum('bhnqc,bchd->bnqhd', p[..., n_loc:], vc))
        return o.reshape(B_, GRID_W, H * dh)

    out = lax.map(row_step, jnp.arange(rows))
    return out.transpose(1, 0, 2, 3).reshape(B_, L, H * dh)


def diff_attend(q, k, v, lam):
    s = jnp.einsum('bqhcd,bkhcd->bhcqk', q, k).astype(F32) * q.shape[-1] ** -0.5
    p = jax.nn.softmax(s, axis=-1)
    w = p[:, :, 0] - lam * p[:, :, 1]
    return jnp.einsum('bhqk,bkhe->bqhe', w.astype(v.dtype), v)


def window_attention(q, k, v, kc, vc, sink):
    B_, L, H, dh = q.shape
    KVH = k.shape[2]
    G = H // KVH
    C = kc.shape[1]
    scale = dh ** -0.5
    nb = L // SWA_BLOCK
    pad = ((0, 0), (SWA_WINDOW, SWA_WINDOW), (0, 0), (0, 0))
    kp, vp = jnp.pad(k, pad), jnp.pad(v, pad)
    q_blocks = q.reshape(B_, nb, SWA_BLOCK, KVH, G, dh).transpose(1, 0, 2, 3, 4, 5)
    sink_col = sink.astype(F32).reshape(1, KVH, G, 1, 1)
    offs = jnp.arange(SWA_SPAN) - SWA_WINDOW
    in_band = jnp.abs(offs[None, :] - jnp.arange(SWA_BLOCK)[:, None]) <= SWA_WINDOW

    def block_step(args):
        n, qb = args
        start = n * SWA_BLOCK
        ks = lax.dynamic_slice_in_dim(kp, start, SWA_SPAN, axis=1)
        vs = lax.dynamic_slice_in_dim(vp, start, SWA_SPAN, axis=1)
        kpos = start + offs
        valid = in_band & ((kpos >= 0) & (kpos < L))[None, :]
        s_loc = jnp.where(valid, jnp.einsum('bqkgd,bskd->bkgqs', qb, ks).astype(F32) * scale, NEG_INF)
        s_ctx = jnp.einsum('bqkgd,bckd->bkgqc', qb, kc).astype(F32) * scale
        sinks = jnp.broadcast_to(sink_col, s_ctx.shape[:-1] + (1,))
        p = jax.nn.softmax(jnp.concatenate([s_loc, s_ctx, sinks], axis=-1), axis=-1).astype(v.dtype)
        o = (jnp.einsum('bkgqs,bskd->bqkgd', p[..., :SWA_SPAN], vs)
             + jnp.einsum('bkgqc,bckd->bqkgd', p[..., SWA_SPAN:SWA_SPAN + C], vc))
        return o.reshape(B_, SWA_BLOCK, H * dh)

    out = lax.map(block_step, (jnp.arange(nb), q_blocks))
    return out.transpose(1, 0, 2, 3).reshape(B_, L, H * dh)


def even_mixer(h, hc, w_in, lam_re, lam_im, log_dt, b_re, b_im, c_re, c_im, d_skip, w_glu, rpb, with_ctx):
    B_, L, _ = h.shape
    C = hc.shape[1]
    z = (h @ w_in).reshape(B_, L, 4, HALF_MIX)
    zc = (hc @ w_in).reshape(B_, C, 4, HALF_MIX)

    def heads(t):
        return t.reshape(t.shape[:2] + (NA_HEADS, NA_HEAD_DIM))

    ya, yac = s5_mixer(z[:, :, 0], zc[:, :, 0], lam_re, lam_im, log_dt, b_re, b_im, c_re, c_im,
                       d_skip, w_glu, with_ctx)
    yb = neighborhood_attention(heads(z[:, :, 1]), heads(z[:, :, 2]), heads(z[:, :, 3]),
                                heads(zc[:, :, 2]), heads(zc[:, :, 3]), rpb)
    y = jnp.concatenate([ya, yb], axis=-1)
    if not with_ctx:
        return y, None
    ybc = ctx_self_attention(heads(zc[:, :, 1]), heads(zc[:, :, 2]), heads(zc[:, :, 3]))
    return y, jnp.concatenate([yac, ybc], axis=-1)


def odd_mixer(h, hc, w_in, diff_lambda, subln_g, sink, lam_init, cos, sin, with_ctx):
    B_, L, _ = h.shape

    def parts(t):
        b, T = t.shape[:2]
        dq, dk, dv, wq, wk, wv = jnp.split(t, OD_CUTS, axis=-1)
        return (dq.reshape(b, T, DIFF_HEADS, 2, DIFF_HEAD_DIM), dk.reshape(b, T, DIFF_HEADS, 2, DIFF_HEAD_DIM),
                dv.reshape(b, T, DIFF_HEADS, 2 * DIFF_HEAD_DIM), wq.reshape(b, T, SWA_HEADS, SWA_HEAD_DIM),
                wk.reshape(b, T, SWA_KV_HEADS, SWA_HEAD_DIM), wv.reshape(b, T, SWA_KV_HEADS, SWA_HEAD_DIM))

    dq, dk, dv, wq, wk, wv = parts(h @ w_in)
    dqc, dkc, dvc, wqc, wkc, wvc = parts(hc @ w_in)

    def rope_pairs(t):
        return apply_rope_2d(t.reshape(t.shape[:2] + (2 * DIFF_HEADS, DIFF_HEAD_DIM)), cos, sin).reshape(t.shape)

    dq, dk = rope_pairs(dq), rope_pairs(dk)
    wq, wk = apply_rope_2d(wq, cos, sin), apply_rope_2d(wk, cos, sin)

    lf = diff_lambda.astype(F32)
    lam = jnp.exp(jnp.sum(lf[0] * lf[1])) - jnp.exp(jnp.sum(lf[2] * lf[3])) + lam_init

    def post(o):
        return (rms_norm(o, subln_g) * (1.0 - lam_init)).reshape(o.shape[:2] + (HALF_MIX,))

    k_all = jnp.concatenate([dkc, dk], axis=1)
    v_all = jnp.concatenate([dvc, dv], axis=1)
    nb = L // Q_BLOCK
    q_blocks = dq.reshape(B_, nb, Q_BLOCK, DIFF_HEADS, 2, DIFF_HEAD_DIM).swapaxes(0, 1)
    oc = lax.map(lambda qb: diff_attend(qb, k_all, v_all, lam), q_blocks)
    y_c = post(oc.swapaxes(0, 1).reshape(B_, L, DIFF_HEADS, 2 * DIFF_HEAD_DIM))
    y_d = window_attention(wq, wk, wv, wkc, wvc, sink)
    y = jnp.concatenate([y_c, y_d], axis=-1)
    if not with_ctx:
        return y, None
    yc_c = post(diff_attend(dqc, dkc, dvc, lam))
    yc_d = ctx_self_attention(wqc, wkc, wvc, sink)
    return y, jnp.concatenate([yc_c, yc_d], axis=-1)


def conv_ffn(h, w_up, dw_w, dw_b, w_down):
    u = h @ w_up
    u = lax.conv_general_dilated(u, dw_w[:, None, :].astype(u.dtype), (1,),
                                 [(CONV_W // 2, CONV_W // 2)],
                                 dimension_numbers=('NWC', 'WIO', 'NWC'),
                                 feature_group_count=u.shape[-1]) + dw_b
    gate, val = jnp.split(u, 2, axis=-1)
    return (jax.nn.silu(gate) * val) @ w_down


def setup_inputs(seed: int = 0) -> dict:
    key = jax.random.key(seed)
    ks = jax.random.split(key, 32)

    def nrm(k, shape, scale):
        return jax.random.normal(k, shape, F32) * scale

    def gain(k, shape):
        return 1.0 + nrm(k, shape, 0.02)

    D, F = D_MODEL, FFN_HIDDEN
    s5_shape = (N_EVEN, 2, S5_GROUPS, S5_STATE)
    n_idx = jnp.arange(S5_STATE, dtype=F32)
    return {
        'x': nrm(ks[0], (BATCH, SEQ, D), 1.0),
        'c': nrm(ks[1], (BATCH, D), 1.0),
        'ctx': nrm(ks[2], (BATCH, CTX_LEN, D), 1.0),
        'c_ctx': nrm(ks[3], (D,), 1.0),
        'ada_w': nrm(ks[4], (DEPTH, D, 6 * D), 0.5 * D ** -0.5),
        'ada_b': nrm(ks[5], (DEPTH, 6 * D), 0.01),
        'norm_mix_g': gain(ks[6], (DEPTH, D)),
        'norm_ffn_g': gain(ks[7], (DEPTH, D)),
        'w_out': nrm(ks[8], (DEPTH, MIX_WIDTH, D), MIX_WIDTH ** -0.5),
        'ffn_w_up': nrm(ks[9], (DEPTH, D, 2 * F), D ** -0.5),
        'ffn_dw_w': nrm(ks[10], (DEPTH, CONV_W, 2 * F), CONV_W ** -0.5),
        'ffn_dw_b': nrm(ks[11], (DEPTH, 2 * F), 0.01),
        'ffn_w_down': nrm(ks[12], (DEPTH, F, D), F ** -0.5),
        'ev_w_in': nrm(ks[13], (N_EVEN, D, EV_IN), D ** -0.5),
        's5_lam_re': -0.5 + nrm(ks[14], s5_shape, 0.01),
        's5_lam_im': math.pi * n_idx + nrm(ks[15], s5_shape, 0.01),
        's5_log_dt': jax.random.uniform(ks[16], (N_EVEN, 2, S5_GROUPS), F32,
                                        math.log(S5_DT_MIN), math.log(S5_DT_MAX)),
        's5_b_re': nrm(ks[17], s5_shape + (S5_GROUP_CH,), (2 * S5_GROUP_CH) ** -0.5),
        's5_b_im': nrm(ks[18], s5_shape + (S5_GROUP_CH,), (2 * S5_GROUP_CH) ** -0.5),
        's5_c_re': nrm(ks[19], (N_EVEN, 2, S5_GROUPS, S5_GROUP_CH, S5_STATE), S5_STATE ** -0.5),
        's5_c_im': nrm(ks[20], (N_EVEN, 2, S5_GROUPS, S5_GROUP_CH, S5_STATE), S5_STATE ** -0.5),
        's5_d': nrm(ks[21], (N_EVEN, S5_CH), 1.0),
        's5_w_glu': nrm(ks[22], (N_EVEN, S5_CH, S5_CH), S5_CH ** -0.5),
        'na_rpb': nrm(ks[23], (N_EVEN, NA_HEADS, 2 * NA_ROWS - 1, 2 * NA_COLS - 1), 0.1),
        'od_w_in': nrm(ks[24], (N_ODD, D, OD_IN), D ** -0.5),
        'diff_lambda': nrm(ks[25], (N_ODD, 4, DIFF_HEAD_DIM), 0.1),
        'diff_subln_g': gain(ks[26], (N_ODD, 2 * DIFF_HEAD_DIM)),
        'swa_sink': nrm(ks[27], (N_ODD, SWA_HEADS), 1.0),
        'final_norm_g': gain(ks[28], (D,)),
    }


def reference(x, c, ctx, c_ctx, ada_w, ada_b, norm_mix_g, norm_ffn_g, w_out, ffn_w_up, ffn_dw_w,
              ffn_dw_b, ffn_w_down, ev_w_in, s5_lam_re, s5_lam_im, s5_log_dt, s5_b_re, s5_b_im,
              s5_c_re, s5_c_im, s5_d, s5_w_glu, na_rpb, od_w_in, diff_lambda, diff_subln_g, swa_sink,
              final_norm_g):
    L = x.shape[1]
    cos, sin = rope_2d_tables(L)
    xc = ctx
    s_lat, s_ctx = jax.nn.silu(c), jax.nn.silu(c_ctx)
    for l in range(DEPTH):
        with_ctx = l < DEPTH - 1
        mod = (s_lat @ ada_w[l] + ada_b[l])[:, None, :]
        modc = s_ctx @ ada_w[l] + ada_b[l]
        sh1, sc1, g1, sh2, sc2, g2 = jnp.split(mod, 6, axis=-1)
        sh1c, sc1c, g1c, sh2c, sc2c, g2c = jnp.split(modc, 6, axis=-1)
        h = rms_norm(x, norm_mix_g[l]) * (1 + sc1) + sh1
        hc = rms_norm(xc, norm_mix_g[l]) * (1 + sc1c) + sh1c
        i = l // 2
        if l % 2 == 0:
            y, yc = even_mixer(h, hc, ev_w_in[i], s5_lam_re[i], s5_lam_im[i], s5_log_dt[i], s5_b_re[i],
                               s5_b_im[i], s5_c_re[i], s5_c_im[i], s5_d[i], s5_w_glu[i], na_rpb[i], with_ctx)
        else:
            lam_init = 0.8 - 0.6 * math.exp(-0.3 * l)
            y, yc = odd_mixer(h, hc, od_w_in[i], diff_lambda[i], diff_subln_g[i], swa_sink[i],
                              lam_init, cos, sin, with_ctx)
        x = x + g1 * (y @ w_out[l])
        h = rms_norm(x, norm_ffn_g[l]) * (1 + sc2) + sh2
        x = x + g2 * conv_ffn(h, ffn_w_up[l], ffn_dw_w[l], ffn_dw_b[l], ffn_w_down[l])
        if with_ctx:
            xc = xc + g1c * (yc @ w_out[l])
            hc = rms_norm(xc, norm_ffn_g[l]) * (1 + sc2c) + sh2c
            xc = xc + g2c * conv_ffn(hc, ffn_w_up[l], ffn_dw_w[l], ffn_dw_b[l], ffn_w_down[l])
    return rms_norm(x, final_norm_g)
```

```python
import functools
import math

import jax
import jax.numpy as jnp
import numpy as np
from jax import lax
from jax.experimental import pallas as pl
from jax.experimental.pallas import tpu as pltpu

F32 = jnp.float32
BF16 = jnp.bfloat16

D_MODEL = 2048
HALF_MIX = D_MODEL // 2
FFN_HIDDEN = 5632
GRID_W = 64
HEAD_DIM = 128
NORM_EPS = 1e-6
NEG_INF = -1e30
ROPE_BASE = 10000.0
ROPE_FREQS = 32

S5_GROUPS = 64
S5_GROUP_CH = 16
S5_STATE = 64
S5_BLOCKS = S5_GROUPS * S5_STATE // 128
S5_CHUNK = 128
S5_PITCH = S5_CHUNK + 8
S5_UNROLL = 8

NA_ROWS = 8
NA_COLS = 16
NA_QROWS = 4
NA_KROWS = 12

SWA_WINDOW = 128
SWA_BLOCK = 128

MOD_ROWS = 32
VMEM_LIMIT_V7X = 56 * 1024 * 1024


def _params(*sem):
    return pltpu.CompilerParams(dimension_semantics=sem, vmem_limit_bytes=VMEM_LIMIT_V7X)


def _resident(shape, index_map):
    return pl.BlockSpec(shape, index_map, pipeline_mode=pl.Buffered(1))


def _sigmoid(x):
    return 1.0 / (1.0 + jnp.exp(-x))


def _dot(a, b):
    return jnp.dot(a, b, preferred_element_type=F32)


def _dot_nt(a, b):
    return lax.dot_general(a, b, (((1,), (1,)), ((), ())), preferred_element_type=F32)


def _modnorm(x, g, shift, scale):
    ms = jnp.mean(x * x, axis=-1, keepdims=True)
    y = x * lax.rsqrt(ms + NORM_EPS) * g
    return y * (1.0 + scale) + shift


def _ada_kernel(s_ref, w_ref, b_ref, o_ref):
    s = s_ref[...]
    s = s * _sigmoid(s)
    o_ref[...] = _dot(s.astype(BF16), w_ref[...].astype(BF16)) + b_ref[...]


def _ada_call(s_all, ada_w, ada_b):
    depth, d, n6 = ada_w.shape
    tn = 1024
    return pl.pallas_call(
        _ada_kernel,
        grid=(depth, n6 // tn),
        in_specs=[pl.BlockSpec((MOD_ROWS, d), lambda l, n: (0, 0)),
                  pl.BlockSpec((None, d, tn), lambda l, n: (l, 0, n)),
                  pl.BlockSpec((None, 1, tn), lambda l, n: (l, 0, n))],
        out_specs=pl.BlockSpec((None, MOD_ROWS, tn), lambda l, n: (l, 0, n)),
        out_shape=jax.ShapeDtypeStruct((depth, MOD_ROWS, n6), F32),
        compiler_params=_params("arbitrary", "arbitrary"),
        name="ada_mod",
    )(s_all, ada_w, ada_b.reshape(depth, 1, n6))


def _rope(a, cosf, sinf, first_half):
    fwd = pltpu.roll(a, HEAD_DIM - ROPE_FREQS, 1)
    bwd = pltpu.roll(a, ROPE_FREQS, 1)
    return a * cosf + jnp.where(first_half, fwd, bwd) * sinf


def _proj_kernel(x_ref, mod_ref, g_ref, w_ref, *rest, n_out, chunk, rope_heads):
    if rope_heads is not None:
        cos_ref, sin_ref, o_ref, h_scr = rest
    else:
        o_ref, h_scr = rest
    h_scr[...] = _modnorm(x_ref[...], g_ref[...], mod_ref[0:1, :], mod_ref[1:2, :]).astype(BF16)
    if rope_heads is not None:
        cosf, sinf = cos_ref[...], sin_ref[...]
        lane = lax.broadcasted_iota(jnp.int32, cosf.shape, 1)
        first_half = (lane % (2 * ROPE_FREQS)) < ROPE_FREQS
    for j in range(n_out // chunk):
        acc = _dot(h_scr[...], w_ref[:, j * chunk:(j + 1) * chunk])
        for hh in range(chunk // HEAD_DIM):
            head = j * (chunk // HEAD_DIM) + hh
            a = acc[:, hh * HEAD_DIM:(hh + 1) * HEAD_DIM]
            if rope_heads is not None and head in rope_heads:
                a = _rope(a, cosf, sinf, first_half)
            o_ref[:, head * HEAD_DIM:(head + 1) * HEAD_DIM] = a.astype(BF16)


def _proj_call(x, mod_l, g, w_all, layer_idx, is_ctx, rope=None, rope_heads=None):
    b_, s_, d = x.shape
    n_out = w_all.shape[2]
    tm = 256
    mod_row = (lambda b: b_) if is_ctx else (lambda b: b)
    in_specs = [pl.BlockSpec((None, tm, d), lambda b, i: (b, i, 0)),
                pl.BlockSpec((None, 6, d), lambda b, i: (mod_row(b), 0, 0)),
                pl.BlockSpec((1, d), lambda b, i: (0, 0)),
                _resident((None, d, n_out), lambda b, i: (layer_idx, 0, 0))]
    args = [x, mod_l, g, w_all]
    if rope is not None:
        in_specs += [pl.BlockSpec((tm, HEAD_DIM), lambda b, i: (i, 0)),
                     pl.BlockSpec((tm, HEAD_DIM), lambda b, i: (i, 0))]
        args += list(rope)
    else:
        rope_heads = None
    return pl.pallas_call(
        functools.partial(_proj_kernel, n_out=n_out, chunk=512, rope_heads=rope_heads),
        grid=(b_, s_ // tm),
        in_specs=in_specs,
        out_specs=pl.BlockSpec((None, tm, n_out), lambda b, i: (b, i, 0)),
        out_shape=jax.ShapeDtypeStruct((b_, s_, n_out), BF16),
        scratch_shapes=[pltpu.VMEM((tm, d), BF16)],
        compiler_params=_params("arbitrary", "arbitrary"),
        name="proj_ctx" if is_ctx else "proj_lat",
    )(*args)


def _outproj_kernel(ya_ref, yb_ref, x_ref, mod_ref, wa_ref, wb_ref, o_ref):
    acc = _dot(ya_ref[...], wa_ref[...]) + _dot(yb_ref[...], wb_ref[...])
    o_ref[...] = x_ref[...] + mod_ref[2:3, :] * acc


def _outproj_call(ya, yb, x, mod_l, w_all, layer_idx, is_ctx):
    b_, s_, d = x.shape
    tm = 256
    half = ya.shape[2]
    mod_row = (lambda b: b_) if is_ctx else (lambda b: b)
    return pl.pallas_call(
        _outproj_kernel,
        grid=(b_, s_ // tm),
        in_specs=[pl.BlockSpec((None, tm, half), lambda b, i: (b, i, 0)),
                  pl.BlockSpec((None, tm, half), lambda b, i: (b, i, 0)),
                  pl.BlockSpec((None, tm, d), lambda b, i: (b, i, 0)),
                  pl.BlockSpec((None, 6, d), lambda b, i: (mod_row(b), 0, 0)),
                  _resident((None, half, d), lambda b, i: (layer_idx, 0, 0)),
                  _resident((None, half, d), lambda b, i: (layer_idx, 1, 0))],
        out_specs=pl.BlockSpec((None, tm, d), lambda b, i: (b, i, 0)),
        out_shape=jax.ShapeDtypeStruct(x.shape, F32),
        compiler_params=_params("arbitrary", "arbitrary"),
        name="outproj_ctx" if is_ctx else "outproj_lat",
    )(ya, yb, x, mod_l, w_all, w_all)


FFN_HALO = 16


def _ffn_kernel(x_ref, xp_ref, xn_ref, mod_ref, g_ref, wg_ref, wv_ref, cwg_ref, cwv_ref, cbg_ref, cbv_ref,
                wd_ref, o_ref, h_scr, acc_scr, *, tm):
    i = pl.program_id(1)
    f = pl.program_id(2)
    rows = tm + 2 * FFN_HALO

    @pl.when(f == 0)
    def _():
        g, shift, scale = g_ref[...], mod_ref[3:4, :], mod_ref[4:5, :]
        h_scr[FFN_HALO:FFN_HALO + tm, :] = _modnorm(x_ref[...], g, shift, scale).astype(BF16)
        hp = jnp.where(i > 0, _modnorm(xp_ref[...], g, shift, scale), 0.0)
        hn = jnp.where(i < pl.num_programs(1) - 1, _modnorm(xn_ref[...], g, shift, scale), 0.0)
        h_scr[0:FFN_HALO, :] = hp.astype(BF16)
        h_scr[FFN_HALO + tm:rows, :] = hn.astype(BF16)
        acc_scr[...] = jnp.zeros_like(acc_scr)

    h = h_scr[...]

    def conv(u, cw_ref, cb_ref):
        prev = pltpu.roll(u, 1, 0)
        nxt = pltpu.roll(u, rows - 1, 0)
        c = cw_ref[0:1, :] * prev + cw_ref[1:2, :] * u + cw_ref[2:3, :] * nxt + cb_ref[...]
        return c[FFN_HALO:FFN_HALO + tm, :]

    gate = conv(_dot(h, wg_ref[...]), cwg_ref, cbg_ref)
    val = conv(_dot(h, wv_ref[...]), cwv_ref, cbv_ref)
    z = gate * _sigmoid(gate) * val
    acc_scr[...] += _dot(z.astype(BF16), wd_ref[...])

    @pl.when(f == pl.num_programs(2) - 1)
    def _():
        o_ref[...] = x_ref[...] + mod_ref[5:6, :] * acc_scr[...]


def _ffn_call(x, mod_l, g, w_up, dw_w, dw_b, w_down, layer_idx, is_ctx):
    b_, s_, d = x.shape
    hid = w_down.shape[1]
    tm = min(512, s_)
    tf = 512
    n_f = hid // tf
    nh = tm // FFN_HALO
    last_halo = s_ // FFN_HALO - 1
    mod_row = (lambda b: b_) if is_ctx else (lambda b: b)
    return pl.pallas_call(
        functools.partial(_ffn_kernel, tm=tm),
        grid=(b_, s_ // tm, n_f),
        in_specs=[pl.BlockSpec((None, tm, d), lambda b, i, f: (b, i, 0)),
                  pl.BlockSpec((None, FFN_HALO, d), lambda b, i, f: (b, jnp.maximum(i * nh - 1, 0), 0)),
                  pl.BlockSpec((None, FFN_HALO, d), lambda b, i, f: (b, jnp.minimum((i + 1) * nh, last_halo), 0)),
                  pl.BlockSpec((None, 6, d), lambda b, i, f: (mod_row(b), 0, 0)),
                  pl.BlockSpec((1, d), lambda b, i, f: (0, 0)),
                  pl.BlockSpec((None, d, tf), lambda b, i, f: (layer_idx, 0, f)),
                  pl.BlockSpec((None, d, tf), lambda b, i, f: (layer_idx, 0, n_f + f)),
                  pl.BlockSpec((None, 3, tf), lambda b, i, f: (layer_idx, 0, f)),
                  pl.BlockSpec((None, 3, tf), lambda b, i, f: (layer_idx, 0, n_f + f)),
                  pl.BlockSpec((None, 1, tf), lambda b, i, f: (layer_idx, 0, f)),
                  pl.BlockSpec((None, 1, tf), lambda b, i, f: (layer_idx, 0, n_f + f)),
                  pl.BlockSpec((None, tf, d), lambda b, i, f: (layer_idx, f, 0))],
        out_specs=pl.BlockSpec((None, tm, d), lambda b, i, f: (b, i, 0)),
        out_shape=jax.ShapeDtypeStruct(x.shape, F32),
        scratch_shapes=[pltpu.VMEM((tm + 2 * FFN_HALO, d), BF16), pltpu.VMEM((tm, d), F32)],
        compiler_params=_params("arbitrary", "arbitrary", "arbitrary"),
        name="ffn_ctx" if is_ctx else "ffn_lat",
    )(x, x, x, mod_l, g, w_up, w_up, dw_w, dw_w, dw_b, dw_b, w_down)


def _final_norm_kernel(x_ref, g_ref, o_ref):
    x = x_ref[...]
    ms = jnp.mean(x * x, axis=-1, keepdims=True)
    o_ref[...] = x * lax.rsqrt(ms + NORM_EPS) * g_ref[...]


def _final_norm_call(x, g):
    b_, s_, d = x.shape
    tm = 512
    return pl.pallas_call(
        _final_norm_kernel,
        grid=(b_, s_ // tm),
        in_specs=[pl.BlockSpec((None, tm, d), lambda b, i: (b, i, 0)),
                  pl.BlockSpec((1, d), lambda b, i: (0, 0))],
        out_specs=pl.BlockSpec((None, tm, d), lambda b, i: (b, i, 0)),
        out_shape=jax.ShapeDtypeStruct(x.shape, F32),
        compiler_params=_params("arbitrary", "arbitrary"),
        name="final_norm",
    )(x, g)


def _s5_kernel(ucf_ref, ulf_ref, ucb_ref, ulb_ref, wb_ref, wc_ref, lam_ref,
               ycf_ref, ylf_ref, ycb_ref, ylb_ref, st_scr, h_scr, *, n_cc):
    j = pl.program_id(1)
    tc = S5_CHUNK

    @pl.when(j == 0)
    def _():
        h_scr[...] = jnp.zeros_like(h_scr)

    def drive(u_ref, r):
        for c in range(8):
            res = _dot(u_ref[:, c * 128:(c + 1) * 128], wb_ref[r, c])
            for sl in range(4):
                row0 = (4 * c + sl) * S5_PITCH
                st_scr[2 * r, row0:row0 + tc, :] = res[:, sl * 256:sl * 256 + 128]
                st_scr[2 * r + 1, row0:row0 + tc, :] = res[:, sl * 256 + 128:(sl + 1) * 256]

    @pl.when(j < n_cc)
    def _():
        drive(ucf_ref, 0)
        drive(ucb_ref, 1)

    @pl.when(j >= n_cc)
    def _():
        drive(ulf_ref, 0)
        drive(ulb_ref, 1)

    lam = [lam_ref[k] for k in range(4)]

    def step(carry, t, r):
        hr, hi = carry
        lr, li = lam[2 * r], lam[2 * r + 1]
        idx = pl.ds(t, S5_BLOCKS, stride=S5_PITCH)
        nr = lr * hr - li * hi + st_scr[2 * r, idx, :]
        ni = lr * hi + li * hr + st_scr[2 * r + 1, idx, :]
        st_scr[2 * r, idx, :] = nr
        st_scr[2 * r + 1, idx, :] = ni
        return nr, ni

    def body(it, carry):
        cf, cb = carry[:2], carry[2:]
        for jj in range(S5_UNROLL):
            t = it * S5_UNROLL + jj
            cf = step(cf, t, 0)
            cb = step(cb, tc - 1 - t, 1)
        return cf + cb

    init = tuple(h_scr[k] for k in range(4))
    fin = lax.fori_loop(0, tc // S5_UNROLL, body, init)
    for k in range(4):
        h_scr[k] = fin[k]

    def readout(y_ref, r):
        for c in range(8):
            parts = []
            for sl in range(4):
                row0 = (4 * c + sl) * S5_PITCH
                parts.append(st_scr[2 * r, row0:row0 + tc, :].astype(BF16))
                parts.append(st_scr[2 * r + 1, row0:row0 + tc, :].astype(BF16))
            y_ref[:, c * 128:(c + 1) * 128] = _dot(jnp.concatenate(parts, axis=1), wc_ref[r, c])

    @pl.when(j < n_cc)
    def _():
        readout(ycf_ref, 0)
        readout(ycb_ref, 1)

    @pl.when(j >= n_cc)
    def _():
        readout(ylf_ref, 0)
        readout(ylb_ref, 1)


def _s5_call(z_ctx, z_lat, wb, wc, lam):
    b_, c_len, _ = z_ctx.shape
    l_len = z_lat.shape[1]
    tc = S5_CHUNK
    n_cc, n_lc = c_len // tc, l_len // tc
    ch = HALF_MIX

    def cf(b, j): return (b, jnp.minimum(j, n_cc - 1), 0)
    def lf(b, j): return (b, jnp.maximum(j - n_cc, 0), 0)
    def cb(b, j): return (b, jnp.maximum(n_cc - 1 - j, 0), 0)
    def lb(b, j): return (b, jnp.minimum(n_lc - 1 - (j - n_cc), n_lc - 1), 0)

    def spec(m): return pl.BlockSpec((None, tc, ch), m)

    return pl.pallas_call(
        functools.partial(_s5_kernel, n_cc=n_cc),
        grid=(b_, n_cc + n_lc),
        in_specs=[spec(cf), spec(lf), spec(cb), spec(lb),
                  _resident(wb.shape, lambda b, j: (0, 0, 0, 0)),
                  _resident(wc.shape, lambda b, j: (0, 0, 0, 0)),
                  _resident(lam.shape, lambda b, j: (0, 0, 0))],
        out_specs=[spec(cf), spec(lf), spec(cb), spec(lb)],
        out_shape=[jax.ShapeDtypeStruct((b_, c_len, ch), F32), jax.ShapeDtypeStruct((b_, l_len, ch), F32),
                   jax.ShapeDtypeStruct((b_, c_len, ch), F32), jax.ShapeDtypeStruct((b_, l_len, ch), F32)],
        scratch_shapes=[pltpu.VMEM((4, S5_BLOCKS * S5_PITCH, 128), F32), pltpu.VMEM((4, S5_BLOCKS, 128), F32)],
        compiler_params=_params("arbitrary", "arbitrary"),
        name="s5_scan",
    )(z_ctx, z_lat, z_ctx, z_lat, wb, wc, lam)


def _s5_pack(lam_re, lam_im, log_dt, b_re, b_im, c_re, c_im):
    lam_re = jnp.minimum(lam_re.astype(F32), -1e-4)
    lam_im = lam_im.astype(F32)
    dt = jnp.exp(log_dt.astype(F32))[..., None]
    mag = jnp.exp(lam_re * dt)
    lb_re, lb_im = mag * jnp.cos(lam_im * dt), mag * jnp.sin(lam_im * dt)
    den = lam_re * lam_re + lam_im * lam_im
    num_re = lb_re - 1.0
    coef_re = (num_re * lam_re + lb_im * lam_im) / den
    coef_im = (lb_im * lam_re - num_re * lam_im) / den
    b_re, b_im = b_re.astype(F32), b_im.astype(F32)
    bb_re = coef_re[..., None] * b_re - coef_im[..., None] * b_im
    bb_im = coef_re[..., None] * b_im + coef_im[..., None] * b_re
    lam = jnp.stack([lb_re[0], lb_im[0], lb_re[1], lb_im[1]]).reshape(4, S5_BLOCKS, 128)
    eye4, eye2 = jnp.eye(4, dtype=F32), jnp.eye(2, dtype=F32)
    bb = jnp.stack([bb_re, bb_im], axis=1).reshape(2, 2, 8, 4, 2, S5_STATE, S5_GROUP_CH)
    wb = jnp.einsum('rxcsgph,ts,kg->rctkhsxgp', bb, eye4, eye2).reshape(2, 8, 128, 1024)
    cc = jnp.stack([c_re.astype(F32), -c_im.astype(F32)], axis=1).reshape(2, 2, 8, 4, 2, S5_GROUP_CH, S5_STATE)
    wc = jnp.einsum('rxcsghp,ts,kg->rcsxgptkh', cc, eye4, eye2).reshape(2, 8, 1024, 128)
    return wb.astype(BF16), wc.astype(BF16), lam


def _glu_kernel(yf_ref, yb_ref, u_ref, d_ref, w_ref, o_ref):
    y = yf_ref[...] + yb_ref[...] + d_ref[...] * u_ref[...].astype(F32)
    t = 0.5 * y * (1.0 + jnp.tanh(math.sqrt(2.0 / math.pi) * (y + 0.044715 * (y * y * y))))
    gate = _dot(t.astype(BF16), w_ref[...])
    o_ref[...] = (t * _sigmoid(gate)).astype(BF16)


def _glu_call(yf, yb, z, d_skip, w_all, layer_idx):
    b_, s_, ch = yf.shape
    tm = 256
    return pl.pallas_call(
        _glu_kernel,
        grid=(b_, s_ // tm),
        in_specs=[pl.BlockSpec((None, tm, ch), lambda b, i: (b, i, 0)),
                  pl.BlockSpec((None, tm, ch), lambda b, i: (b, i, 0)),
                  pl.BlockSpec((None, tm, ch), lambda b, i: (b, i, 0)),
                  pl.BlockSpec((1, ch), lambda b, i: (0, 0)),
                  _resident((None, ch, ch), lambda b, i: (layer_idx, 0, 0))],
        out_specs=pl.BlockSpec((None, tm, ch), lambda b, i: (b, i, 0)),
        out_shape=jax.ShapeDtypeStruct((b_, s_, ch), BF16),
        compiler_params=_params("arbitrary", "arbitrary"),
        name="s5_glu",
    )(yf, yb, z, d_skip, w_all)


def _attend(q, segs, sink=None):
    scale = HEAD_DIM ** -0.5
    logits = []
    for k, _, bias, mask in segs:
        s = _dot_nt(q, k) * scale
        if bias is not None:
            s = s + bias
        if mask is not None:
            s = jnp.where(mask, s, NEG_INF)
        logits.append(s)
    m = functools.reduce(jnp.maximum, [jnp.max(s, axis=-1, keepdims=True) for s in logits])
    if sink is not None:
        m = jnp.maximum(m, sink)
    denom = jnp.exp(sink - m) if sink is not None else 0.0
    out = 0.0
    for s, (_, v, _, _) in zip(logits, segs):
        p = jnp.exp(s - m)
        denom = denom + jnp.sum(p, axis=-1, keepdims=True)
        out = out + _dot(p.astype(BF16), v)
    return out / denom


def _head(ref, h, width=HEAD_DIM):
    return ref[:, h * width:(h + 1) * width]


def _stack_heads(q_ref, kh, group):
    return jnp.concatenate([_head(q_ref, kh * group + g) for g in range(group)], axis=0)


def _sink_column(sink_ref, kh, group, rows):
    return jnp.concatenate([jnp.full((rows, 1), sink_ref[kh * group + g], F32) for g in range(group)], axis=0)


def _unstack_heads(o_ref, o, kh, group, rows):
    for g in range(group):
        h = kh * group + g
        o_ref[:, h * HEAD_DIM:(h + 1) * HEAD_DIM] = o[g * rows:(g + 1) * rows].astype(o_ref.dtype)


def _dense_attn_kernel(q_ref, k_ref, v_ref, *rest, kv_heads, group, has_sink):
    if has_sink:
        sink_ref, o_ref = rest
    else:
        (o_ref,) = rest
    rows = q_ref.shape[0]
    for kh in range(kv_heads):
        q = _stack_heads(q_ref, kh, group)
        sink = _sink_column(sink_ref, kh, group, rows) if has_sink else None
        o = _attend(q, [(_head(k_ref, kh), _head(v_ref, kh), None, None)], sink)
        _unstack_heads(o_ref, o, kh, group, rows)


def _dense_ctx_attn_call(z_ctx, q_col, k_col, v_col, kv_heads, group, sink=None):
    b_, c_len, _ = z_ctx.shape
    qw, kw = kv_heads * group * HEAD_DIM, kv_heads * HEAD_DIM
    in_specs = [pl.BlockSpec((None, c_len, qw), lambda b: (b, 0, q_col)),
                pl.BlockSpec((None, c_len, kw), lambda b: (b, 0, k_col)),
                pl.BlockSpec((None, c_len, kw), lambda b: (b, 0, v_col))]
    args = [z_ctx, z_ctx, z_ctx]
    if sink is not None:
        in_specs.append(pl.BlockSpec(memory_space=pltpu.SMEM))
        args.append(sink)
    return pl.pallas_call(
        functools.partial(_dense_attn_kernel, kv_heads=kv_heads, group=group, has_sink=sink is not None),
        grid=(b_,),
        in_specs=in_specs,
        out_specs=pl.BlockSpec((None, c_len, qw), lambda b: (b, 0, 0)),
        out_shape=jax.ShapeDtypeStruct((b_, c_len, qw), BF16),
        compiler_params=_params("arbitrary"),
        name="ctx_attn",
    )(*args)


def _na_kernel(q_ref, k0_ref, k1_ref, k2_ref, v0_ref, v1_ref, v2_ref, kc_ref, vc_ref, bias_ref, o_ref):
    kb = k0_ref.shape[0]
    for h in range(HALF_MIX // HEAD_DIM):
        segs = [(_head(k_ref, h), _head(v_ref, h), bias_ref[h, :, t * kb:(t + 1) * kb], None)
                for t, (k_ref, v_ref) in enumerate(((k0_ref, v0_ref), (k1_ref, v1_ref), (k2_ref, v2_ref)))]
        segs.append((_head(kc_ref, h), _head(vc_ref, h), None, None))
        o = _attend(_head(q_ref, h), segs)
        o_ref[:, h * HEAD_DIM:(h + 1) * HEAD_DIM] = o.astype(o_ref.dtype)


def _na_bias_tables(rpb, rows):
    tables = []
    a = np.arange(NA_QROWS)[:, None]
    c = np.arange(NA_KROWS)[None, :]
    qc = np.arange(GRID_W)[:, None]
    kc = np.arange(GRID_W)[None, :]
    ws = np.clip(qc - NA_COLS // 2, 0, GRID_W - NA_COLS)
    col_ok = (kc >= ws) & (kc < ws + NA_COLS)
    dc = np.clip(kc - qc + NA_COLS - 1, 0, 2 * NA_COLS - 2)
    for r0 in (0, NA_QROWS, rows - NA_QROWS):
        base = int(np.clip(r0 - NA_ROWS // 2, 0, rows - NA_KROWS))
        qrow, krow = r0 + a, base + c
        rs = np.clip(qrow - NA_ROWS // 2, 0, rows - NA_ROWS)
        row_ok = (krow >= rs) & (krow < rs + NA_ROWS)
        dr = np.clip(krow - qrow + NA_ROWS - 1, 0, 2 * NA_ROWS - 2)
        vals = rpb.astype(F32)[:, dr][:, :, :, dc]
        ok = row_ok[:, :, None, None] & col_ok[None, None, :, :]
        t = jnp.where(ok[None], vals, NEG_INF).transpose(0, 1, 3, 2, 4)
        tables.append(t.reshape(rpb.shape[0], NA_QROWS * GRID_W, NA_KROWS * GRID_W))
    return jnp.stack(tables)


def _na_call(z_lat, z_ctx, bias):
    b_, l_len, _ = z_lat.shape
    c_len = z_ctx.shape[1]
    qb = NA_QROWS * GRID_W
    nblk = l_len // qb
    w = HALF_MIX

    def kmap(t, col):
        return lambda b, i: (b, jnp.clip(i - 1, 0, nblk - 3) + t, col)

    def variant(b, i):
        return (jnp.where(i == 0, 0, jnp.where(i == nblk - 1, 2, 1)), 0, 0, 0)

    return pl.pallas_call(
        _na_kernel,
        grid=(b_, nblk),
        in_specs=[pl.BlockSpec((None, qb, w), lambda b, i: (b, i, 1))]
                 + [pl.BlockSpec((None, qb, w), kmap(t, 2)) for t in range(3)]
                 + [pl.BlockSpec((None, qb, w), kmap(t, 3)) for t in range(3)]
                 + [pl.BlockSpec((None, c_len, w), lambda b, i: (b, 0, 2)),
                    pl.BlockSpec((None, c_len, w), lambda b, i: (b, 0, 3)),
                    pl.BlockSpec((None,) + bias.shape[1:], variant)],
        out_specs=pl.BlockSpec((None, qb, w), lambda b, i: (b, i, 0)),
        out_shape=jax.ShapeDtypeStruct((b_, l_len, w), BF16),
        compiler_params=_params("arbitrary", "arbitrary"),
        name="na_attn",
    )(z_lat, z_lat, z_lat, z_lat, z_lat, z_lat, z_lat, z_ctx, z_ctx, bias)


def _diff_kernel(q_ref, kc_ref, vc_ref, *rest, has_lat, out_scale):
    if has_lat:
        kl_ref, vl_ref, lam_ref, g_ref, o_ref = rest
    else:
        lam_ref, g_ref, o_ref = rest
    lam = lam_ref[0]
    for h in range(HALF_MIX // (2 * HEAD_DIM)):
        outs = []
        for c in range(2):
            segs = [(_head(kc_ref, 2 * h + c), _head(vc_ref, h, 2 * HEAD_DIM), None, None)]
            if has_lat:
                segs.append((_head(kl_ref, 2 * h + c), _head(vl_ref, h, 2 * HEAD_DIM), None, None))
            outs.append(_attend(_head(q_ref, 2 * h + c), segs))
        o = outs[0] - lam * outs[1]
        ms = jnp.mean(o * o, axis=-1, keepdims=True)
        y = o * lax.rsqrt(ms + NORM_EPS) * g_ref[...] * out_scale
        o_ref[:, h * 2 * HEAD_DIM:(h + 1) * 2 * HEAD_DIM] = y.astype(o_ref.dtype)


def _diff_call(z_q, z_ctx, z_lat, lam, subln_g, out_scale):
    b_, s_, _ = z_q.shape
    c_len = z_ctx.shape[1]
    tq = 256
    w = HALF_MIX
    in_specs = [pl.BlockSpec((None, tq, w), lambda b, i: (b, i, 0)),
                pl.BlockSpec((None, c_len, w), lambda b, i: (b, 0, 1)),
                pl.BlockSpec((None, c_len, w), lambda b, i: (b, 0, 2))]
    args = [z_q, z_ctx, z_ctx]
    if z_lat is not None:
        l_len = z_lat.shape[1]
        in_specs += [pl.BlockSpec((None, l_len, w), lambda b, i: (b, 0, 1)),
                     pl.BlockSpec((None, l_len, w), lambda b, i: (b, 0, 2))]
        args += [z_lat, z_lat]
    in_specs += [pl.BlockSpec(memory_space=pltpu.SMEM), pl.BlockSpec((1, 2 * HEAD_DIM), lambda b, i: (0, 0))]
    args += [lam, subln_g]
    return pl.pallas_call(
        functools.partial(_diff_kernel, has_lat=z_lat is not None, out_scale=out_scale),
        grid=(b_, s_ // tq),
        in_specs=in_specs,
        out_specs=pl.BlockSpec((None, tq, w), lambda b, i: (b, i, 0)),
        out_shape=jax.ShapeDtypeStruct((b_, s_, w), BF16),
        compiler_params=_params("arbitrary", "arbitrary"),
        name="diff_attn_lat" if z_lat is not None else "diff_attn_ctx",
    )(*args)


def _swa_kernel(q_ref, kp_ref, kn0_ref, kn_ref, vp_ref, vn0_ref, vn_ref, kc_ref, vc_ref, sink_ref, o_ref,
                *, kv_heads, group):
    n = pl.program_id(1)
    nb = pl.num_programs(1)
    blk = SWA_BLOCK
    qi = lax.broadcasted_iota(jnp.int32, (group * blk, blk), 0) % blk
    kj = lax.broadcasted_iota(jnp.int32, (group * blk, blk), 1)
    prev_ok = (kj >= qi) & (n > 0)
    next_ok = (kj <= qi) & (n < nb - 1)
    for kh in range(kv_heads):
        q = _stack_heads(q_ref, kh, group)
        segs = [(_head(kp_ref, kh), _head(vp_ref, kh), None, prev_ok),
                (_head(kn0_ref, kh), _head(vn0_ref, kh), None, None),
                (_head(kn_ref, kh), _head(vn_ref, kh), None, next_ok),
                (_head(kc_ref, kh), _head(vc_ref, kh), None, None)]
        o = _attend(q, segs, _sink_column(sink_ref, kh, group, blk))
        _unstack_heads(o_ref, o, kh, group, blk)


def _swa_call(z_lat, z_ctx, sink, kv_heads=2, group=4):
    b_, l_len, _ = z_lat.shape
    c_len = z_ctx.shape[1]
    assert SWA_WINDOW == SWA_BLOCK
    blk = SWA_BLOCK
    nb = l_len // blk
    qw, kw = kv_heads * group * HEAD_DIM, kv_heads * HEAD_DIM
    q_col = 3 * HALF_MIX // qw
    k_col = 4 * HALF_MIX // kw
    v_col = k_col + 1

    def prev(col): return lambda b, n: (b, jnp.maximum(n - 1, 0), col)
    def cur(col): return lambda b, n: (b, n, col)
    def nxt(col): return lambda b, n: (b, jnp.minimum(n + 1, nb - 1), col)

    return pl.pallas_call(
        functools.partial(_swa_kernel, kv_heads=kv_heads, group=group),
        grid=(b_, nb),
        in_specs=[pl.BlockSpec((None, blk, qw), lambda b, n: (b, n, q_col))]
                 + [pl.BlockSpec((None, blk, kw), m(k_col)) for m in (prev, cur, nxt)]
                 + [pl.BlockSpec((None, blk, kw), m(v_col)) for m in (prev, cur, nxt)]
                 + [pl.BlockSpec((None, c_len, kw), lambda b, n: (b, 0, k_col)),
                    pl.BlockSpec((None, c_len, kw), lambda b, n: (b, 0, v_col)),
                    pl.BlockSpec(memory_space=pltpu.SMEM)],
        out_specs=pl.BlockSpec((None, blk, qw), lambda b, n: (b, n, 0)),
        out_shape=jax.ShapeDtypeStruct((b_, l_len, qw), BF16),
        compiler_params=_params("arbitrary", "arbitrary"),
        name="swa_attn",
    )(z_lat, z_lat, z_lat, z_lat, z_lat, z_lat, z_lat, z_ctx, z_ctx, sink)


def _rope_tables(l_len):
    t = jnp.arange(l_len, dtype=jnp.int32)
    pos = jnp.stack([t // GRID_W, t % GRID_W], axis=-1).astype(F32)
    inv = ROPE_BASE ** (-2.0 * jnp.arange(ROPE_FREQS, dtype=F32) / (2 * ROPE_FREQS))
    ang = pos[:, :, None] * inv
    cos, sin = jnp.cos(ang), jnp.sin(ang)
    cosf = jnp.concatenate([cos[:, 0], cos[:, 0], cos[:, 1], cos[:, 1]], axis=-1)
    sinf = jnp.concatenate([-sin[:, 0], sin[:, 0], -sin[:, 1], sin[:, 1]], axis=-1)
    return cosf, sinf


_OD_ROPE_HEADS = frozenset(list(range(0, 16)) + list(range(24, 34)))


def kernel(x, c, ctx, c_ctx, ada_w, ada_b, norm_mix_g, norm_ffn_g, w_out, ffn_w_up, ffn_dw_w, ffn_dw_b,
           ffn_w_down, ev_w_in, s5_lam_re, s5_lam_im, s5_log_dt, s5_b_re, s5_b_im, s5_c_re, s5_c_im, s5_d,
           s5_w_glu, na_rpb, od_w_in, diff_lambda, diff_subln_g, swa_sink, final_norm_g):
    b_, l_len, d = x.shape
    depth = ada_w.shape[0]
    assert d == D_MODEL and b_ < MOD_ROWS
    rows = l_len // GRID_W

    w_out_h = w_out.astype(BF16)
    w_up_h = ffn_w_up.astype(BF16)
    w_down_h = ffn_w_down.astype(BF16)
    ev_w_h = ev_w_in.astype(BF16)
    od_w_h = od_w_in.astype(BF16)
    glu_w_h = s5_w_glu.astype(BF16)
    dw_b = ffn_dw_b.reshape(depth, 1, -1)

    s_all = jnp.zeros((MOD_ROWS, d), F32).at[:b_].set(c).at[b_].set(c_ctx)
    mod = _ada_call(s_all, ada_w, ada_b).reshape(depth, MOD_ROWS, 6, d)
    rope = _rope_tables(l_len)

    xc = ctx
    for l in range(depth):
        with_ctx = l < depth - 1
        i = l // 2
        mod_l = mod[l]
        g_mix = norm_mix_g[l].reshape(1, d)
        g_ffn = norm_ffn_g[l].reshape(1, d)
        if l % 2 == 0:
            z_lat = _proj_call(x, mod_l, g_mix, ev_w_h, i, False)
            z_ctx = _proj_call(xc, mod_l, g_mix, ev_w_h, i, True)
            wb, wc, lam = _s5_pack(s5_lam_re[i], s5_lam_im[i], s5_log_dt[i], s5_b_re[i], s5_b_im[i],
                                   s5_c_re[i], s5_c_im[i])
            ycf, ylf, ycb, ylb = _s5_call(z_ctx, z_lat, wb, wc, lam)
            d_skip = s5_d[i].reshape(1, HALF_MIX)
            ya = _glu_call(ylf, ylb, z_lat, d_skip, glu_w_h, i)
            yb = _na_call(z_lat, z_ctx, _na_bias_tables(na_rpb[i], rows))
            if with_ctx:
                yac = _glu_call(ycf, ycb, z_ctx, d_skip, glu_w_h, i)
                ybc = _dense_ctx_attn_call(z_ctx, 1, 2, 3, HALF_MIX // HEAD_DIM, 1)
        else:
            lam_init = 0.8 - 0.6 * math.exp(-0.3 * l)
            z_lat = _proj_call(x, mod_l, g_mix, od_w_h, i, False, rope, _OD_ROPE_HEADS)
            z_ctx = _proj_call(xc, mod_l, g_mix, od_w_h, i, True)
            lf = diff_lambda[i].astype(F32)
            lam = (jnp.exp(jnp.sum(lf[0] * lf[1])) - jnp.exp(jnp.sum(lf[2] * lf[3])) + lam_init).reshape(1)
            subln = diff_subln_g[i].reshape(1, 2 * HEAD_DIM)
            sink = swa_sink[i].astype(F32)
            ya = _diff_call(z_lat, z_ctx, z_lat, lam, subln, 1.0 - lam_init)
            yb = _swa_call(z_lat, z_ctx, sink)
            if with_ctx:
                yac = _diff_call(z_ctx, z_ctx, None, lam, subln, 1.0 - lam_init)
                ybc = _dense_ctx_attn_call(z_ctx, 3, 16, 17, 2, 4, sink)
        x = _outproj_call(ya, yb, x, mod_l, w_out_h, l, False)
        x = _ffn_call(x, mod_l, g_ffn, w_up_h, ffn_dw_w, dw_b, w_down_h, l, False)
        if with_ctx:
            xc = _outproj_call(yac, ybc, xc, mod_l, w_out_h, l, True)
            xc = _ffn_call(xc, mod_l, g_ffn, w_up_h, ffn_dw_w, dw_b, w_down_h, l, True)
    return _final_norm_call(x, final_norm_g.reshape(1, d))
```

```python
import functools
import math

import jax
import jax.numpy as jnp
import numpy as np
from jax import lax
from jax.experimental import pallas as pl
from jax.experimental.pallas import tpu as pltpu

F32 = jnp.float32
BF16 = jnp.bfloat16

D_MODEL = 2048
HALF_MIX = D_MODEL // 2
FFN_HIDDEN = 5632
GRID_W = 64
HEAD_DIM = 128
NORM_EPS = 1e-6
NEG_INF = -1e30
ROPE_BASE = 10000.0
ROPE_FREQS = 32
LOG2_E = math.log2(math.e)

S5_GROUPS = 64
S5_GROUP_CH = 16
S5_STATE = 64
S5_BLOCKS = S5_GROUPS * S5_STATE // 128
S5_CHUNK = 128
S5_TPITCH = 2 * S5_BLOCKS + 4
S5_UNROLL = 8

NA_ROWS = 8
NA_COLS = 16
NA_QROWS = 4
NA_KROWS = 12

SWA_WINDOW = 128
SWA_BLOCK = 128

MOD_ROWS = 32
VMEM_LIMIT_V7X = 56 * 1024 * 1024


def _params(*sem):
    return pltpu.CompilerParams(dimension_semantics=sem, vmem_limit_bytes=VMEM_LIMIT_V7X)


def _resident(shape, index_map):
    return pl.BlockSpec(shape, index_map, pipeline_mode=pl.Buffered(1))


def _sigmoid(x):
    return 1.0 / (1.0 + jnp.exp(-x))


def _dot(a, b):
    return jnp.dot(a, b, preferred_element_type=F32)


def _dot_nt(a, b):
    return lax.dot_general(a, b, (((1,), (1,)), ((), ())), preferred_element_type=F32)


def _modnorm(x, g, shift, scale):
    ms = jnp.mean(x * x, axis=-1, keepdims=True)
    y = x * lax.rsqrt(ms + NORM_EPS) * g
    return y * (1.0 + scale) + shift


def _ada_kernel(s_ref, w_ref, b_ref, o_ref):
    s = s_ref[...]
    s = s * _sigmoid(s)
    o_ref[...] = _dot(s.astype(BF16), w_ref[...].astype(BF16)) + b_ref[...]


def _ada_call(s_all, ada_w, ada_b):
    depth, d, n6 = ada_w.shape
    tn = 1024
    return pl.pallas_call(
        _ada_kernel,
        grid=(depth, n6 // tn),
        in_specs=[pl.BlockSpec((MOD_ROWS, d), lambda l, n: (0, 0)),
                  pl.BlockSpec((None, d, tn), lambda l, n: (l, 0, n)),
                  pl.BlockSpec((None, 1, tn), lambda l, n: (l, 0, n))],
        out_specs=pl.BlockSpec((None, MOD_ROWS, tn), lambda l, n: (l, 0, n)),
        out_shape=jax.ShapeDtypeStruct((depth, MOD_ROWS, n6), F32),
        compiler_params=_params("arbitrary", "arbitrary"),
        name="ada_mod",
    )(s_all, ada_w, ada_b.reshape(depth, 1, n6))


def _rope(a, cosf, sinf, first_half):
    fwd = pltpu.roll(a, HEAD_DIM - ROPE_FREQS, 1)
    bwd = pltpu.roll(a, ROPE_FREQS, 1)
    return a * cosf + jnp.where(first_half, fwd, bwd) * sinf


def _proj_kernel(x_ref, mod_ref, g_ref, w_ref, *rest, n_out, chunk, rope_heads):
    if rope_heads is not None:
        cos_ref, sin_ref, o_ref, h_scr = rest
    else:
        o_ref, h_scr = rest
    h_scr[...] = _modnorm(x_ref[...], g_ref[...], mod_ref[0:1, :], mod_ref[1:2, :]).astype(BF16)
    if rope_heads is not None:
        cosf, sinf = cos_ref[...], sin_ref[...]
        lane = lax.broadcasted_iota(jnp.int32, cosf.shape, 1)
        first_half = (lane % (2 * ROPE_FREQS)) < ROPE_FREQS
    for j in range(n_out // chunk):
        acc = _dot(h_scr[...], w_ref[:, j * chunk:(j + 1) * chunk])
        for hh in range(chunk // HEAD_DIM):
            head = j * (chunk // HEAD_DIM) + hh
            a = acc[:, hh * HEAD_DIM:(hh + 1) * HEAD_DIM]
            if rope_heads is not None and head in rope_heads:
                a = _rope(a, cosf, sinf, first_half)
            o_ref[:, head * HEAD_DIM:(head + 1) * HEAD_DIM] = a.astype(BF16)


def _proj_call(x, mod_l, g, w_all, layer_idx, is_ctx, rope=None, rope_heads=None):
    b_, s_, d = x.shape
    n_out = w_all.shape[2]
    tm = 256
    mod_row = (lambda b: b_) if is_ctx else (lambda b: b)
    in_specs = [pl.BlockSpec((None, tm, d), lambda b, i: (b, i, 0)),
                pl.BlockSpec((None, 6, d), lambda b, i: (mod_row(b), 0, 0)),
                pl.BlockSpec((1, d), lambda b, i: (0, 0)),
                _resident((None, d, n_out), lambda b, i: (layer_idx, 0, 0))]
    args = [x, mod_l, g, w_all]
    if rope is not None:
        in_specs += [pl.BlockSpec((tm, HEAD_DIM), lambda b, i: (i, 0)),
                     pl.BlockSpec((tm, HEAD_DIM), lambda b, i: (i, 0))]
        args += list(rope)
    else:
        rope_heads = None
    return pl.pallas_call(
        functools.partial(_proj_kernel, n_out=n_out, chunk=512, rope_heads=rope_heads),
        grid=(b_, s_ // tm),
        in_specs=in_specs,
        out_specs=pl.BlockSpec((None, tm, n_out), lambda b, i: (b, i, 0)),
        out_shape=jax.ShapeDtypeStruct((b_, s_, n_out), BF16),
        scratch_shapes=[pltpu.VMEM((tm, d), BF16)],
        compiler_params=_params("arbitrary", "arbitrary"),
        name="proj_ctx" if is_ctx else "proj_lat",
    )(*args)


def _outproj_kernel(ya_ref, yb_ref, x_ref, mod_ref, wa_ref, wb_ref, o_ref):
    acc = _dot(ya_ref[...], wa_ref[...]) + _dot(yb_ref[...], wb_ref[...])
    o_ref[...] = x_ref[...] + mod_ref[2:3, :] * acc


def _outproj_call(ya, yb, x, mod_l, w_all, layer_idx, is_ctx):
    b_, s_, d = x.shape
    tm = 256
    half = ya.shape[2]
    mod_row = (lambda b: b_) if is_ctx else (lambda b: b)
    return pl.pallas_call(
        _outproj_kernel,
        grid=(b_, s_ // tm),
        in_specs=[pl.BlockSpec((None, tm, half), lambda b, i: (b, i, 0)),
                  pl.BlockSpec((None, tm, half), lambda b, i: (b, i, 0)),
                  pl.BlockSpec((None, tm, d), lambda b, i: (b, i, 0)),
                  pl.BlockSpec((None, 6, d), lambda b, i: (mod_row(b), 0, 0)),
                  _resident((None, half, d), lambda b, i: (layer_idx, 0, 0)),
                  _resident((None, half, d), lambda b, i: (layer_idx, 1, 0))],
        out_specs=pl.BlockSpec((None, tm, d), lambda b, i: (b, i, 0)),
        out_shape=jax.ShapeDtypeStruct(x.shape, F32),
        compiler_params=_params("arbitrary", "arbitrary"),
        name="outproj_ctx" if is_ctx else "outproj_lat",
    )(ya, yb, x, mod_l, w_all, w_all)


FFN_HALO = 16
FFN_SUB = 256


def _ffn_kernel(x_ref, xp_ref, xn_ref, mod_ref, g_ref, fg_ref, wg_ref, wv_ref, cwg_ref, cwv_ref, cbg_ref, cbv_ref,
                wd_ref, o_ref, h_scr, acc_scr, ug_scr, uv_scr, *, tm, final_norm):
    i = pl.program_id(1)
    f = pl.program_id(2)
    rows = tm + 2 * FFN_HALO

    @pl.when(f == 0)
    def _():
        g, shift, scale = g_ref[...], mod_ref[3:4, :], mod_ref[4:5, :]
        h_scr[FFN_HALO:FFN_HALO + tm, :] = _modnorm(x_ref[...], g, shift, scale).astype(BF16)
        hp = jnp.where(i > 0, _modnorm(xp_ref[...], g, shift, scale), 0.0)
        hn = jnp.where(i < pl.num_programs(1) - 1, _modnorm(xn_ref[...], g, shift, scale), 0.0)
        h_scr[0:FFN_HALO, :] = hp.astype(BF16)
        h_scr[FFN_HALO + tm:rows, :] = hn.astype(BF16)
        acc_scr[...] = jnp.zeros_like(acc_scr)

    h = h_scr[...]
    n_sub = wd_ref.shape[0] // FFN_SUB

    def up(s):
        cols = slice(s * FFN_SUB, (s + 1) * FFN_SUB)
        ug_scr[s] = _dot(h, wg_ref[:, cols])
        uv_scr[s] = _dot(h, wv_ref[:, cols])

    def conv(u_scr, s, cw_ref, cb_ref, cols):
        prev = u_scr[s, FFN_HALO - 1:FFN_HALO - 1 + tm, :]
        cur = u_scr[s, FFN_HALO:FFN_HALO + tm, :]
        nxt = u_scr[s, FFN_HALO + 1:FFN_HALO + 1 + tm, :]
        return cw_ref[0:1, cols] * prev + cw_ref[1:2, cols] * cur + cw_ref[2:3, cols] * nxt + cb_ref[:, cols]

    def down(s):
        cols = slice(s * FFN_SUB, (s + 1) * FFN_SUB)
        gate = conv(ug_scr, s, cwg_ref, cbg_ref, cols)
        val = conv(uv_scr, s, cwv_ref, cbv_ref, cols)
        z = gate * _sigmoid(gate) * val
        acc_scr[...] += _dot(z.astype(BF16), wd_ref[cols, :])

    up(0)
    for s in range(1, n_sub):
        up(s)
        down(s - 1)
    down(n_sub - 1)

    @pl.when(f == pl.num_programs(2) - 1)
    def _():
        y = x_ref[...] + mod_ref[5:6, :] * acc_scr[...]
        if final_norm:
            ms = jnp.mean(y * y, axis=-1, keepdims=True)
            y = y * lax.rsqrt(ms + NORM_EPS) * fg_ref[...]
        o_ref[...] = y


def _ffn_call(x, mod_l, g, w_up, dw_w, dw_b, w_down, layer_idx, is_ctx, final_g=None):
    b_, s_, d = x.shape
    hid = w_down.shape[1]
    tm = min(512, s_)
    tf = 512
    n_f = hid // tf
    nh = tm // FFN_HALO
    last_halo = s_ // FFN_HALO - 1
    mod_row = (lambda b: b_) if is_ctx else (lambda b: b)
    return pl.pallas_call(
        functools.partial(_ffn_kernel, tm=tm, final_norm=final_g is not None),
        grid=(b_, s_ // tm, n_f),
        in_specs=[pl.BlockSpec((None, tm, d), lambda b, i, f: (b, i, 0)),
                  pl.BlockSpec((None, FFN_HALO, d), lambda b, i, f: (b, jnp.maximum(i * nh - 1, 0), 0)),
                  pl.BlockSpec((None, FFN_HALO, d), lambda b, i, f: (b, jnp.minimum((i + 1) * nh, last_halo), 0)),
                  pl.BlockSpec((None, 6, d), lambda b, i, f: (mod_row(b), 0, 0)),
                  pl.BlockSpec((1, d), lambda b, i, f: (0, 0)),
                  pl.BlockSpec((1, d), lambda b, i, f: (0, 0)),
                  pl.BlockSpec((None, d, tf), lambda b, i, f: (layer_idx, 0, f)),
                  pl.BlockSpec((None, d, tf), lambda b, i, f: (layer_idx, 0, n_f + f)),
                  pl.BlockSpec((None, 3, tf), lambda b, i, f: (layer_idx, 0, f)),
                  pl.BlockSpec((None, 3, tf), lambda b, i, f: (layer_idx, 0, n_f + f)),
                  pl.BlockSpec((None, 1, tf), lambda b, i, f: (layer_idx, 0, f)),
                  pl.BlockSpec((None, 1, tf), lambda b, i, f: (layer_idx, 0, n_f + f)),
                  pl.BlockSpec((None, tf, d), lambda b, i, f: (layer_idx, f, 0))],
        out_specs=pl.BlockSpec((None, tm, d), lambda b, i, f: (b, i, 0)),
        out_shape=jax.ShapeDtypeStruct(x.shape, F32),
        scratch_shapes=[pltpu.VMEM((tm + 2 * FFN_HALO, d), BF16), pltpu.VMEM((tm, d), F32),
                        pltpu.VMEM((tf // FFN_SUB, tm + 2 * FFN_HALO, FFN_SUB), F32),
                        pltpu.VMEM((tf // FFN_SUB, tm + 2 * FFN_HALO, FFN_SUB), F32)],
        compiler_params=_params("arbitrary", "arbitrary", "arbitrary"),
        name="ffn_ctx" if is_ctx else "ffn_lat",
    )(x, x, x, mod_l, g, g if final_g is None else final_g, w_up, w_up, dw_w, dw_w, dw_b, dw_b, w_down)


def _rows_of_step(t, part):
    return pl.ds(t * S5_TPITCH + part, S5_BLOCKS, stride=2)


def _rows_of_block(s, part):
    return pl.ds(2 * s + part, S5_CHUNK, stride=S5_TPITCH)


def _s5_kernel(ucf_ref, ulf_ref, ucb_ref, ulb_ref, wb_ref, wc_ref, lam_ref,
               ycf_ref, ylf_ref, ycb_ref, ylb_ref, bu_f, bu_b, st_f, st_b, h_scr, *, n_cc):
    j = pl.program_id(1)
    bu, st = (bu_f, bu_b), (st_f, st_b)

    @pl.when(j == 0)
    def _():
        h_scr[...] = jnp.zeros_like(h_scr)

    def drive(u_ref, r):
        for c in range(8):
            res = _dot(u_ref[:, c * 128:(c + 1) * 128], wb_ref[r, c])
            for sl in range(4):
                for part in range(2):
                    col = sl * 256 + part * 128
                    bu[r][_rows_of_block(4 * c + sl, part), :] = res[:, col:col + 128]

    @pl.when(j < n_cc)
    def _():
        drive(ucf_ref, 0)
        drive(ucb_ref, 1)

    @pl.when(j >= n_cc)
    def _():
        drive(ulf_ref, 0)
        drive(ulb_ref, 1)

    lam = [lam_ref[k] for k in range(4)]

    def step(carry, t, r):
        hr, hi = carry
        lr, li = lam[2 * r], lam[2 * r + 1]
        nr = lr * hr - li * hi + bu[r][_rows_of_step(t, 0), :]
        ni = lr * hi + li * hr + bu[r][_rows_of_step(t, 1), :]
        st[r][_rows_of_step(t, 0), :] = nr
        st[r][_rows_of_step(t, 1), :] = ni
        return nr, ni

    def body(it, carry):
        cf, cb = carry[:2], carry[2:]
        for jj in range(S5_UNROLL):
            t = it * S5_UNROLL + jj
            cf = step(cf, t, 0)
            cb = step(cb, S5_CHUNK - 1 - t, 1)
        return cf + cb

    init = tuple(h_scr[k] for k in range(4))
    fin = lax.fori_loop(0, S5_CHUNK // S5_UNROLL, body, init)
    for k in range(4):
        h_scr[k] = fin[k]

    def readout(y_ref, r):
        for c in range(8):
            parts = [st[r][_rows_of_block(4 * c + sl, part), :].astype(BF16)
                     for sl in range(4) for part in range(2)]
            y_ref[:, c * 128:(c + 1) * 128] = _dot(jnp.concatenate(parts, axis=1), wc_ref[r, c])

    @pl.when(j < n_cc)
    def _():
        readout(ycf_ref, 0)
        readout(ycb_ref, 1)

    @pl.when(j >= n_cc)
    def _():
        readout(ylf_ref, 0)
        readout(ylb_ref, 1)


def _s5_call(z_ctx, z_lat, wb, wc, lam):
    b_, c_len, _ = z_ctx.shape
    l_len = z_lat.shape[1]
    tc = S5_CHUNK
    n_cc, n_lc = c_len // tc, l_len // tc
    ch = HALF_MIX

    def cf(b, j): return (b, jnp.minimum(j, n_cc - 1), 0)
    def lf(b, j): return (b, jnp.maximum(j - n_cc, 0), 0)
    def cb(b, j): return (b, jnp.maximum(n_cc - 1 - j, 0), 0)
    def lb(b, j): return (b, jnp.minimum(n_lc - 1 - (j - n_cc), n_lc - 1), 0)

    def spec(m): return pl.BlockSpec((None, tc, ch), m)

    return pl.pallas_call(
        functools.partial(_s5_kernel, n_cc=n_cc),
        grid=(b_, n_cc + n_lc),
        in_specs=[spec(cf), spec(lf), spec(cb), spec(lb),
                  _resident(wb.shape, lambda b, j: (0, 0, 0, 0)),
                  _resident(wc.shape, lambda b, j: (0, 0, 0, 0)),
                  _resident(lam.shape, lambda b, j: (0, 0, 0))],
        out_specs=[spec(cf), spec(lf), spec(cb), spec(lb)],
        out_shape=[jax.ShapeDtypeStruct((b_, c_len, ch), F32), jax.ShapeDtypeStruct((b_, l_len, ch), F32),
                   jax.ShapeDtypeStruct((b_, c_len, ch), F32), jax.ShapeDtypeStruct((b_, l_len, ch), F32)],
        scratch_shapes=[pltpu.VMEM((S5_CHUNK * S5_TPITCH, 128), F32)] * 4 + [pltpu.VMEM((4, S5_BLOCKS, 128), F32)],
        compiler_params=_params("arbitrary", "arbitrary"),
        name="s5_scan",
    )(z_ctx, z_lat, z_ctx, z_lat, wb, wc, lam)


def _s5_pack(lam_re, lam_im, log_dt, b_re, b_im, c_re, c_im):
    lam_re = jnp.minimum(lam_re.astype(F32), -1e-4)
    lam_im = lam_im.astype(F32)
    dt = jnp.exp(log_dt.astype(F32))[..., None]
    mag = jnp.exp(lam_re * dt)
    lb_re, lb_im = mag * jnp.cos(lam_im * dt), mag * jnp.sin(lam_im * dt)
    den = lam_re * lam_re + lam_im * lam_im
    num_re = lb_re - 1.0
    coef_re = (num_re * lam_re + lb_im * lam_im) / den
    coef_im = (lb_im * lam_re - num_re * lam_im) / den
    b_re, b_im = b_re.astype(F32), b_im.astype(F32)
    bb_re = coef_re[..., None] * b_re - coef_im[..., None] * b_im
    bb_im = coef_re[..., None] * b_im + coef_im[..., None] * b_re
    lam = jnp.stack([lb_re[0], lb_im[0], lb_re[1], lb_im[1]]).reshape(4, S5_BLOCKS, 128)
    eye4, eye2 = jnp.eye(4, dtype=F32), jnp.eye(2, dtype=F32)
    bb = jnp.stack([bb_re, bb_im], axis=1).reshape(2, 2, 8, 4, 2, S5_STATE, S5_GROUP_CH)
    wb = jnp.einsum('rxcsgph,ts,kg->rctkhsxgp', bb, eye4, eye2).reshape(2, 8, 128, 1024)
    cc = jnp.stack([c_re.astype(F32), -c_im.astype(F32)], axis=1).reshape(2, 2, 8, 4, 2, S5_GROUP_CH, S5_STATE)
    wc = jnp.einsum('rxcsghp,ts,kg->rcsxgptkh', cc, eye4, eye2).reshape(2, 8, 1024, 128)
    return wb.astype(BF16), wc.astype(BF16), lam


def _glu_kernel(yf_ref, yb_ref, u_ref, d_ref, w_ref, o_ref):
    y = yf_ref[...] + yb_ref[...] + d_ref[...] * u_ref[...].astype(F32)
    t = 0.5 * y * (1.0 + jnp.tanh(math.sqrt(2.0 / math.pi) * (y + 0.044715 * (y * y * y))))
    gate = _dot(t.astype(BF16), w_ref[...])
    o_ref[...] = (t * _sigmoid(gate)).astype(BF16)


def _glu_call(yf, yb, z, d_skip, w_all, layer_idx):
    b_, s_, ch = yf.shape
    tm = 256
    return pl.pallas_call(
        _glu_kernel,
        grid=(b_, s_ // tm),
        in_specs=[pl.BlockSpec((None, tm, ch), lambda b, i: (b, i, 0)),
                  pl.BlockSpec((None, tm, ch), lambda b, i: (b, i, 0)),
                  pl.BlockSpec((None, tm, ch), lambda b, i: (b, i, 0)),
                  pl.BlockSpec((1, ch), lambda b, i: (0, 0)),
                  _resident((None, ch, ch), lambda b, i: (layer_idx, 0, 0))],
        out_specs=pl.BlockSpec((None, tm, ch), lambda b, i: (b, i, 0)),
        out_shape=jax.ShapeDtypeStruct((b_, s_, ch), BF16),
        compiler_params=_params("arbitrary", "arbitrary"),
        name="s5_glu",
    )(yf, yb, z, d_skip, w_all)


def _attend(q, segs, sink=None):
    scale = HEAD_DIM ** -0.5 * LOG2_E
    logits = []
    for k, _, bias, mask in segs:
        s = _dot_nt(q, k) * scale
        if bias is not None:
            s = s + bias
        if mask is not None:
            s = jnp.where(mask, s, NEG_INF)
        logits.append(s)
    m = jnp.max(functools.reduce(jnp.maximum, [_fold_lanes(s, jnp.maximum) for s in logits]),
                axis=-1, keepdims=True)
    if sink is not None:
        m = jnp.maximum(m, sink)
    psum = None
    out = 0.0
    for s, (_, v, _, _) in zip(logits, segs):
        p = jnp.exp2(s - m)
        folded = _fold_lanes(p, jnp.add)
        psum = folded if psum is None else psum + folded
        out = out + _dot(p.astype(BF16), v)
    denom = jnp.sum(psum, axis=-1, keepdims=True)
    if sink is not None:
        denom = denom + jnp.exp2(sink - m)
    return out / denom


def _fold_lanes(x, op):
    return functools.reduce(op, [x[:, i:i + 128] for i in range(0, x.shape[1], 128)])


def _head(ref, h, width=HEAD_DIM):
    return ref[:, h * width:(h + 1) * width]


def _stack_heads(q_ref, kh, group):
    return jnp.concatenate([_head(q_ref, kh * group + g) for g in range(group)], axis=0)


def _sink_column(sink_ref, kh, group, rows):
    return jnp.concatenate([jnp.full((rows, 1), sink_ref[kh * group + g] * LOG2_E, F32) for g in range(group)],
                           axis=0)


def _unstack_heads(o_ref, o, kh, group, rows):
    for g in range(group):
        h = kh * group + g
        o_ref[:, h * HEAD_DIM:(h + 1) * HEAD_DIM] = o[g * rows:(g + 1) * rows].astype(o_ref.dtype)


def _dense_attn_kernel(q_ref, k_ref, v_ref, *rest, kv_heads, group, has_sink):
    if has_sink:
        sink_ref, o_ref = rest
    else:
        (o_ref,) = rest
    rows = q_ref.shape[0]
    for kh in range(kv_heads):
        q = _stack_heads(q_ref, kh, group)
        sink = _sink_column(sink_ref, kh, group, rows) if has_sink else None
        o = _attend(q, [(_head(k_ref, kh), _head(v_ref, kh), None, None)], sink)
        _unstack_heads(o_ref, o, kh, group, rows)


def _dense_ctx_attn_call(z_ctx, q_col, k_col, v_col, kv_heads, group, sink=None):
    b_, c_len, _ = z_ctx.shape
    qw, kw = kv_heads * group * HEAD_DIM, kv_heads * HEAD_DIM
    in_specs = [pl.BlockSpec((None, c_len, qw), lambda b: (b, 0, q_col)),
                pl.BlockSpec((None, c_len, kw), lambda b: (b, 0, k_col)),
                pl.BlockSpec((None, c_len, kw), lambda b: (b, 0, v_col))]
    args = [z_ctx, z_ctx, z_ctx]
    if sink is not None:
        in_specs.append(pl.BlockSpec(memory_space=pltpu.SMEM))
        args.append(sink)
    return pl.pallas_call(
        functools.partial(_dense_attn_kernel, kv_heads=kv_heads, group=group, has_sink=sink is not None),
        grid=(b_,),
        in_specs=in_specs,
        out_specs=pl.BlockSpec((None, c_len, qw), lambda b: (b, 0, 0)),
        out_shape=jax.ShapeDtypeStruct((b_, c_len, qw), BF16),
        compiler_params=_params("arbitrary"),
        name="ctx_attn",
    )(*args)


def _na_kernel(q_ref, k0_ref, k1_ref, k2_ref, v0_ref, v1_ref, v2_ref, kc_ref, vc_ref, bias_ref, o_ref):
    kb = k0_ref.shape[0]
    for h in range(HALF_MIX // HEAD_DIM):
        segs = [(_head(k_ref, h), _head(v_ref, h), bias_ref[h, :, t * kb:(t + 1) * kb], None)
                for t, (k_ref, v_ref) in enumerate(((k0_ref, v0_ref), (k1_ref, v1_ref), (k2_ref, v2_ref)))]
        segs.append((_head(kc_ref, h), _head(vc_ref, h), None, None))
        o = _attend(_head(q_ref, h), segs)
        o_ref[:, h * HEAD_DIM:(h + 1) * HEAD_DIM] = o.astype(o_ref.dtype)


def _na_bias_tables(rpb, rows):
    tables = []
    a = np.arange(NA_QROWS)[:, None]
    c = np.arange(NA_KROWS)[None, :]
    qc = np.arange(GRID_W)[:, None]
    kc = np.arange(GRID_W)[None, :]
    ws = np.clip(qc - NA_COLS // 2, 0, GRID_W - NA_COLS)
    col_ok = (kc >= ws) & (kc < ws + NA_COLS)
    dc = np.clip(kc - qc + NA_COLS - 1, 0, 2 * NA_COLS - 2)
    for r0 in (0, NA_QROWS, rows - NA_QROWS):
        base = int(np.clip(r0 - NA_ROWS // 2, 0, rows - NA_KROWS))
        qrow, krow = r0 + a, base + c
        rs = np.clip(qrow - NA_ROWS // 2, 0, rows - NA_ROWS)
        row_ok = (krow >= rs) & (krow < rs + NA_ROWS)
        dr = np.clip(krow - qrow + NA_ROWS - 1, 0, 2 * NA_ROWS - 2)
        vals = rpb.astype(F32)[:, dr][:, :, :, dc]
        ok = row_ok[:, :, None, None] & col_ok[None, None, :, :]
        t = jnp.where(ok[None], vals * LOG2_E, NEG_INF).transpose(0, 1, 3, 2, 4)
        tables.append(t.reshape(rpb.shape[0], NA_QROWS * GRID_W, NA_KROWS * GRID_W))
    return jnp.stack(tables)


def _na_call(z_lat, z_ctx, bias):
    b_, l_len, _ = z_lat.shape
    c_len = z_ctx.shape[1]
    qb = NA_QROWS * GRID_W
    nblk = l_len // qb
    w = HALF_MIX

    def kmap(t, col):
        return lambda b, i: (b, jnp.clip(i - 1, 0, nblk - 3) + t, col)

    def variant(b, i):
        return (jnp.where(i == 0, 0, jnp.where(i == nblk - 1, 2, 1)), 0, 0, 0)

    return pl.pallas_call(
        _na_kernel,
        grid=(b_, nblk),
        in_specs=[pl.BlockSpec((None, qb, w), lambda b, i: (b, i, 1))]
                 + [pl.BlockSpec((None, qb, w), kmap(t, 2)) for t in range(3)]
                 + [pl.BlockSpec((None, qb, w), kmap(t, 3)) for t in range(3)]
                 + [pl.BlockSpec((None, c_len, w), lambda b, i: (b, 0, 2)),
                    pl.BlockSpec((None, c_len, w), lambda b, i: (b, 0, 3)),
                    pl.BlockSpec((None,) + bias.shape[1:], variant)],
        out_specs=pl.BlockSpec((None, qb, w), lambda b, i: (b, i, 0)),
        out_shape=jax.ShapeDtypeStruct((b_, l_len, w), BF16),
        compiler_params=_params("arbitrary", "arbitrary"),
        name="na_attn",
    )(z_lat, z_lat, z_lat, z_lat, z_lat, z_lat, z_lat, z_ctx, z_ctx, bias)


def _diff_kernel(q_ref, kc_ref, vc_ref, *rest, has_lat, out_scale):
    if has_lat:
        kl_ref, vl_ref, lam_ref, g_ref, o_ref = rest
    else:
        lam_ref, g_ref, o_ref = rest
    lam = lam_ref[0]
    for h in range(HALF_MIX // (2 * HEAD_DIM)):
        outs = []
        for c in range(2):
            segs = [(_head(kc_ref, 2 * h + c), _head(vc_ref, h, 2 * HEAD_DIM), None, None)]
            if has_lat:
                segs.append((_head(kl_ref, 2 * h + c), _head(vl_ref, h, 2 * HEAD_DIM), None, None))
            outs.append(_attend(_head(q_ref, 2 * h + c), segs))
        o = outs[0] - lam * outs[1]
        ms = jnp.mean(o * o, axis=-1, keepdims=True)
        y = o * lax.rsqrt(ms + NORM_EPS) * g_ref[...] * out_scale
        o_ref[:, h * 2 * HEAD_DIM:(h + 1) * 2 * HEAD_DIM] = y.astype(o_ref.dtype)


def _diff_call(z_q, z_ctx, z_lat, lam, subln_g, out_scale):
    b_, s_, _ = z_q.shape
    c_len = z_ctx.shape[1]
    tq = 256
    w = HALF_MIX
    in_specs = [pl.BlockSpec((None, tq, w), lambda b, i: (b, i, 0)),
                pl.BlockSpec((None, c_len, w), lambda b, i: (b, 0, 1)),
                pl.BlockSpec((None, c_len, w), lambda b, i: (b, 0, 2))]
    args = [z_q, z_ctx, z_ctx]
    if z_lat is not None:
        l_len = z_lat.shape[1]
        in_specs += [pl.BlockSpec((None, l_len, w), lambda b, i: (b, 0, 1)),
                     pl.BlockSpec((None, l_len, w), lambda b, i: (b, 0, 2))]
        args += [z_lat, z_lat]
    in_specs += [pl.BlockSpec(memory_space=pltpu.SMEM), pl.BlockSpec((1, 2 * HEAD_DIM), lambda b, i: (0, 0))]
    args += [lam, subln_g]
    return pl.pallas_call(
        functools.partial(_diff_kernel, has_lat=z_lat is not None, out_scale=out_scale),
        grid=(b_, s_ // tq),
        in_specs=in_specs,
        out_specs=pl.BlockSpec((None, tq, w), lambda b, i: (b, i, 0)),
        out_shape=jax.ShapeDtypeStruct((b_, s_, w), BF16),
        compiler_params=_params("arbitrary", "arbitrary"),
        name="diff_attn_lat" if z_lat is not None else "diff_attn_ctx",
    )(*args)


def _swa_kernel(q_ref, kp_ref, kn0_ref, kn_ref, vp_ref, vn0_ref, vn_ref, kc_ref, vc_ref, sink_ref, o_ref,
                *, kv_heads, group):
    n = pl.program_id(1)
    nb = pl.num_programs(1)
    blk = SWA_BLOCK
    qi = lax.broadcasted_iota(jnp.int32, (group * blk, blk), 0) % blk
    kj = lax.broadcasted_iota(jnp.int32, (group * blk, blk), 1)
    prev_ok = (kj >= qi) & (n > 0)
    next_ok = (kj <= qi) & (n < nb - 1)
    for kh in range(kv_heads):
        q = _stack_heads(q_ref, kh, group)
        segs = [(_head(kp_ref, kh), _head(vp_ref, kh), None, prev_ok),
                (_head(kn0_ref, kh), _head(vn0_ref, kh), None, None),
                (_head(kn_ref, kh), _head(vn_ref, kh), None, next_ok),
                (_head(kc_ref, kh), _head(vc_ref, kh), None, None)]
        o = _attend(q, segs, _sink_column(sink_ref, kh, group, blk))
        _unstack_heads(o_ref, o, kh, group, blk)


def _swa_call(z_lat, z_ctx, sink, kv_heads=2, group=4):
    b_, l_len, _ = z_lat.shape
    c_len = z_ctx.shape[1]
    assert SWA_WINDOW == SWA_BLOCK
    blk = SWA_BLOCK
    nb = l_len // blk
    qw, kw = kv_heads * group * HEAD_DIM, kv_heads * HEAD_DIM
    q_col = 3 * HALF_MIX // qw
    k_col = 4 * HALF_MIX // kw
    v_col = k_col + 1

    def prev(col): return lambda b, n: (b, jnp.maximum(n - 1, 0), col)
    def cur(col): return lambda b, n: (b, n, col)
    def nxt(col): return lambda b, n: (b, jnp.minimum(n + 1, nb - 1), col)

    return pl.pallas_call(
        functools.partial(_swa_kernel, kv_heads=kv_heads, group=group),
        grid=(b_, nb),
        in_specs=[pl.BlockSpec((None, blk, qw), lambda b, n: (b, n, q_col))]
                 + [pl.BlockSpec((None, blk, kw), m(k_col)) for m in (prev, cur, nxt)]
                 + [pl.BlockSpec((None, blk, kw), m(v_col)) for m in (prev, cur, nxt)]
                 + [pl.BlockSpec((None, c_len, kw), lambda b, n: (b, 0, k_col)),
                    pl.BlockSpec((None, c_len, kw), lambda b, n: (b, 0, v_col)),
                    pl.BlockSpec(memory_space=pltpu.SMEM)],
        out_specs=pl.BlockSpec((None, blk, qw), lambda b, n: (b, n, 0)),
        out_shape=jax.ShapeDtypeStruct((b_, l_len, qw), BF16),
        compiler_params=_params("arbitrary", "arbitrary"),
        name="swa_attn",
    )(z_lat, z_lat, z_lat, z_lat, z_lat, z_lat, z_lat, z_ctx, z_ctx, sink)


def _rope_tables(l_len):
    t = jnp.arange(l_len, dtype=jnp.int32)
    pos = jnp.stack([t // GRID_W, t % GRID_W], axis=-1).astype(F32)
    inv = ROPE_BASE ** (-2.0 * jnp.arange(ROPE_FREQS, dtype=F32) / (2 * ROPE_FREQS))
    ang = pos[:, :, None] * inv
    cos, sin = jnp.cos(ang), jnp.sin(ang)
    cosf = jnp.concatenate([cos[:, 0], cos[:, 0], cos[:, 1], cos[:, 1]], axis=-1)
    sinf = jnp.concatenate([-sin[:, 0], sin[:, 0], -sin[:, 1], sin[:, 1]], axis=-1)
    return cosf, sinf


_OD_ROPE_HEADS = frozenset(list(range(0, 16)) + list(range(24, 34)))


def kernel(x, c, ctx, c_ctx, ada_w, ada_b, norm_mix_g, norm_ffn_g, w_out, ffn_w_up, ffn_dw_w, ffn_dw_b,
           ffn_w_down, ev_w_in, s5_lam_re, s5_lam_im, s5_log_dt, s5_b_re, s5_b_im, s5_c_re, s5_c_im, s5_d,
           s5_w_glu, na_rpb, od_w_in, diff_lambda, diff_subln_g, swa_sink, final_norm_g):
    b_, l_len, d = x.shape
    depth = ada_w.shape[0]
    assert d == D_MODEL and b_ < MOD_ROWS
    rows = l_len // GRID_W

    w_out_h = w_out.astype(BF16)
    w_up_h = ffn_w_up.astype(BF16)
    w_down_h = ffn_w_down.astype(BF16)
    ev_w_h = ev_w_in.astype(BF16)
    od_w_h = od_w_in.astype(BF16)
    glu_w_h = s5_w_glu.astype(BF16)
    dw_b = ffn_dw_b.reshape(depth, 1, -1)

    s_all = jnp.zeros((MOD_ROWS, d), F32).at[:b_].set(c).at[b_].set(c_ctx)
    mod = _ada_call(s_all, ada_w, ada_b).reshape(depth, MOD_ROWS, 6, d)
    rope = _rope_tables(l_len)

    xc = ctx
    for l in range(depth):
        with_ctx = l < depth - 1
        i = l // 2
        mod_l = mod[l]
        g_mix = norm_mix_g[l].reshape(1, d)
        g_ffn = norm_ffn_g[l].reshape(1, d)
        if l % 2 == 0:
            z_lat = _proj_call(x, mod_l, g_mix, ev_w_h, i, False)
            z_ctx = _proj_call(xc, mod_l, g_mix, ev_w_h, i, True)
            wb, wc, lam = _s5_pack(s5_lam_re[i], s5_lam_im[i], s5_log_dt[i], s5_b_re[i], s5_b_im[i],
                                   s5_c_re[i], s5_c_im[i])
            ycf, ylf, ycb, ylb = _s5_call(z_ctx, z_lat, wb, wc, lam)
            d_skip = s5_d[i].reshape(1, HALF_MIX)
            ya = _glu_call(ylf, ylb, z_lat, d_skip, glu_w_h, i)
            yb = _na_call(z_lat, z_ctx, _na_bias_tables(na_rpb[i], rows))
            if with_ctx:
                yac = _glu_call(ycf, ycb, z_ctx, d_skip, glu_w_h, i)
                ybc = _dense_ctx_attn_call(z_ctx, 1, 2, 3, HALF_MIX // HEAD_DIM, 1)
        else:
            lam_init = 0.8 - 0.6 * math.exp(-0.3 * l)
            z_lat = _proj_call(x, mod_l, g_mix, od_w_h, i, False, rope, _OD_ROPE_HEADS)
            z_ctx = _proj_call(xc, mod_l, g_mix, od_w_h, i, True)
            lf = diff_lambda[i].astype(F32)
            lam = (jnp.exp(jnp.sum(lf[0] * lf[1])) - jnp.exp(jnp.sum(lf[2] * lf[3])) + lam_init).reshape(1)
            subln = diff_subln_g[i].reshape(1, 2 * HEAD_DIM)
            sink = swa_sink[i].astype(F32)
            ya = _diff_call(z_lat, z_ctx, z_lat, lam, subln, 1.0 - lam_init)
            yb = _swa_call(z_lat, z_ctx, sink)
            if with_ctx:
                yac = _diff_call(z_ctx, z_ctx, None, lam, subln, 1.0 - lam_init)
                ybc = _dense_ctx_attn_call(z_ctx, 3, 16, 17, 2, 4, sink)
        x = _outproj_call(ya, yb, x, mod_l, w_out_h, l, False)
        x = _ffn_call(x, mod_l, g_ffn, w_up_h, ffn_dw_w, dw_b, w_down_h, l, False,
                      final_g=None if with_ctx else final_norm_g.reshape(1, d))
        if with_ctx:
            xc = _outproj_call(yac, ybc, xc, mod_l, w_out_h, l, True)
            xc = _ffn_call(xc, mod_l, g_ffn, w_up_h, ffn_dw_w, dw_b, w_down_h, l, True)
    return x
```

```python
import functools
import math

import jax
import jax.numpy as jnp
import numpy as np
from jax import lax
from jax.experimental import pallas as pl
from jax.experimental.pallas import tpu as pltpu

F32 = jnp.float32
BF16 = jnp.bfloat16

D_MODEL = 2048
HALF_MIX = D_MODEL // 2
FFN_HIDDEN = 5632
GRID_W = 64
HEAD_DIM = 128
NORM_EPS = 1e-6
NEG_INF = -1e30
ROPE_BASE = 10000.0
ROPE_FREQS = 32
LOG2_E = math.log2(math.e)

S5_GROUPS = 64
S5_GROUP_CH = 16
S5_STATE = 64
S5_BLOCKS = S5_GROUPS * S5_STATE // 128
S5_CHUNK = 128
S5_TPITCH = 2 * S5_BLOCKS + 4
S5_UNROLL = 8

NA_ROWS = 8
NA_COLS = 16
NA_QROWS = 4
NA_KROWS = 12

SWA_WINDOW = 128
SWA_BLOCK = 128

MOD_ROWS = 32
VMEM_LIMIT_V7X = 56 * 1024 * 1024


def _params(*sem):
    return pltpu.CompilerParams(dimension_semantics=sem, vmem_limit_bytes=VMEM_LIMIT_V7X)


def _resident(shape, index_map):
    return pl.BlockSpec(shape, index_map, pipeline_mode=pl.Buffered(1))


def _sigmoid(x):
    return 1.0 / (1.0 + jnp.exp(-x))


def _dot(a, b):
    return jnp.dot(a, b, preferred_element_type=F32)


def _dot_nt(a, b):
    return lax.dot_general(a, b, (((1,), (1,)), ((), ())), preferred_element_type=F32)


def _modnorm(x, g, shift, scale):
    ms = jnp.mean(x * x, axis=-1, keepdims=True)
    y = x * lax.rsqrt(ms + NORM_EPS) * g
    return y * (1.0 + scale) + shift


def _ada_kernel(s_ref, w_ref, b_ref, o_ref):
    s = s_ref[...]
    s = s * _sigmoid(s)
    o_ref[...] = _dot(s.astype(BF16), w_ref[...].astype(BF16)) + b_ref[...]


def _ada_call(s_all, ada_w, ada_b):
    depth, d, n6 = ada_w.shape
    tn = 1024
    return pl.pallas_call(
        _ada_kernel,
        grid=(depth, n6 // tn),
        in_specs=[pl.BlockSpec((MOD_ROWS, d), lambda l, n: (0, 0)),
                  pl.BlockSpec((None, d, tn), lambda l, n: (l, 0, n)),
                  pl.BlockSpec((None, 1, tn), lambda l, n: (l, 0, n))],
        out_specs=pl.BlockSpec((None, MOD_ROWS, tn), lambda l, n: (l, 0, n)),
        out_shape=jax.ShapeDtypeStruct((depth, MOD_ROWS, n6), F32),
        compiler_params=_params("arbitrary", "arbitrary"),
        name="ada_mod",
    )(s_all, ada_w, ada_b.reshape(depth, 1, n6))


def _rope(a, cosf, sinf, first_half):
    fwd = pltpu.roll(a, HEAD_DIM - ROPE_FREQS, 1)
    bwd = pltpu.roll(a, ROPE_FREQS, 1)
    return a * cosf + jnp.where(first_half, fwd, bwd) * sinf


def _proj_kernel(x_ref, mod_ref, g_ref, w_ref, *rest, n_out, chunk, rope_heads):
    if rope_heads is not None:
        cos_ref, sin_ref, o_ref, h_scr = rest
    else:
        o_ref, h_scr = rest
    h_scr[...] = _modnorm(x_ref[...], g_ref[...], mod_ref[0:1, :], mod_ref[1:2, :]).astype(BF16)
    if rope_heads is not None:
        cosf, sinf = cos_ref[...], sin_ref[...]
        lane = lax.broadcasted_iota(jnp.int32, cosf.shape, 1)
        first_half = (lane % (2 * ROPE_FREQS)) < ROPE_FREQS
    for j in range(n_out // chunk):
        acc = _dot(h_scr[...], w_ref[:, j * chunk:(j + 1) * chunk])
        for hh in range(chunk // HEAD_DIM):
            head = j * (chunk // HEAD_DIM) + hh
            a = acc[:, hh * HEAD_DIM:(hh + 1) * HEAD_DIM]
            if rope_heads is not None and head in rope_heads:
                a = _rope(a, cosf, sinf, first_half)
            o_ref[:, head * HEAD_DIM:(head + 1) * HEAD_DIM] = a.astype(BF16)


def _proj_call(x, mod_l, g, w_all, layer_idx, is_ctx, rope=None, rope_heads=None):
    b_, s_, d = x.shape
    n_out = w_all.shape[2]
    tm = 256
    mod_row = (lambda b: b_) if is_ctx else (lambda b: b)
    in_specs = [pl.BlockSpec((None, tm, d), lambda b, i: (b, i, 0)),
                pl.BlockSpec((None, 6, d), lambda b, i: (mod_row(b), 0, 0)),
                pl.BlockSpec((1, d), lambda b, i: (0, 0)),
                _resident((None, d, n_out), lambda b, i: (layer_idx, 0, 0))]
    args = [x, mod_l, g, w_all]
    if rope is not None:
        in_specs += [pl.BlockSpec((tm, HEAD_DIM), lambda b, i: (i, 0)),
                     pl.BlockSpec((tm, HEAD_DIM), lambda b, i: (i, 0))]
        args += list(rope)
    else:
        rope_heads = None
    return pl.pallas_call(
        functools.partial(_proj_kernel, n_out=n_out, chunk=512, rope_heads=rope_heads),
        grid=(b_, s_ // tm),
        in_specs=in_specs,
        out_specs=pl.BlockSpec((None, tm, n_out), lambda b, i: (b, i, 0)),
        out_shape=jax.ShapeDtypeStruct((b_, s_, n_out), BF16),
        scratch_shapes=[pltpu.VMEM((tm, d), BF16)],
        compiler_params=_params("arbitrary", "arbitrary"),
        name="proj_ctx" if is_ctx else "proj_lat",
    )(*args)


def _outproj_kernel(ya_ref, yb_ref, x_ref, mod_ref, wa_ref, wb_ref, o_ref):
    acc = _dot(ya_ref[...], wa_ref[...]) + _dot(yb_ref[...], wb_ref[...])
    o_ref[...] = x_ref[...] + mod_ref[2:3, :] * acc


def _outproj_call(ya, yb, x, mod_l, w_all, layer_idx, is_ctx):
    b_, s_, d = x.shape
    tm = 256
    half = ya.shape[2]
    mod_row = (lambda b: b_) if is_ctx else (lambda b: b)
    return pl.pallas_call(
        _outproj_kernel,
        grid=(b_, s_ // tm),
        in_specs=[pl.BlockSpec((None, tm, half), lambda b, i: (b, i, 0)),
                  pl.BlockSpec((None, tm, half), lambda b, i: (b, i, 0)),
                  pl.BlockSpec((None, tm, d), lambda b, i: (b, i, 0)),
                  pl.BlockSpec((None, 6, d), lambda b, i: (mod_row(b), 0, 0)),
                  _resident((None, half, d), lambda b, i: (layer_idx, 0, 0)),
                  _resident((None, half, d), lambda b, i: (layer_idx, 1, 0))],
        out_specs=pl.BlockSpec((None, tm, d), lambda b, i: (b, i, 0)),
        out_shape=jax.ShapeDtypeStruct(x.shape, F32),
        compiler_params=_params("arbitrary", "arbitrary"),
        name="outproj_ctx" if is_ctx else "outproj_lat",
    )(ya, yb, x, mod_l, w_all, w_all)


FFN_HALO = 16
FFN_SUB = 256


def _ffn_kernel(x_ref, xp_ref, xn_ref, mod_ref, g_ref, fg_ref, wg_ref, wv_ref, cwg_ref, cwv_ref, cbg_ref, cbv_ref,
                wd_ref, o_ref, h_scr, acc_scr, ug_scr, uv_scr, y_scr, *, tm, final_norm):
    i = pl.program_id(1)
    f = pl.program_id(2)
    rows = tm + 2 * FFN_HALO

    @pl.when(f == 0)
    def _():
        g, shift, scale = g_ref[...], mod_ref[3:4, :], mod_ref[4:5, :]
        h_scr[FFN_HALO:FFN_HALO + tm, :] = _modnorm(x_ref[...], g, shift, scale).astype(BF16)
        hp = jnp.where(i > 0, _modnorm(xp_ref[...], g, shift, scale), 0.0)
        hn = jnp.where(i < pl.num_programs(1) - 1, _modnorm(xn_ref[...], g, shift, scale), 0.0)
        h_scr[0:FFN_HALO, :] = hp.astype(BF16)
        h_scr[FFN_HALO + tm:rows, :] = hn.astype(BF16)
        acc_scr[...] = jnp.zeros_like(acc_scr)

    h = h_scr[...]
    n_sub = wd_ref.shape[0] // FFN_SUB

    lanes = FFN_SUB // 128

    def up(s):
        cols = slice(s * FFN_SUB, (s + 1) * FFN_SUB)
        for u_scr, w_ref in ((ug_scr, wg_ref), (uv_scr, wv_ref)):
            res = _dot(h, w_ref[:, cols])
            for j in range(lanes):
                u_scr[s * lanes + j] = res[:, j * 128:(j + 1) * 128]

    def conv(u_scr, slab, cw_ref, cb_ref):
        cols = slice(slab * 128, (slab + 1) * 128)
        before, even, odd, after = (u_scr[slab, pl.ds(FFN_HALO + k, tm // 2, stride=2), :] for k in (-1, 0, 1, 2))
        w0, w1, w2, b = cw_ref[0:1, cols], cw_ref[1:2, cols], cw_ref[2:3, cols], cb_ref[:, cols]
        return jnp.concatenate([w0 * before + w1 * even + w2 * odd + b,
                                w0 * even + w1 * odd + w2 * after + b], axis=0)

    def down(s):
        zs = []
        for j in range(lanes):
            gate = conv(ug_scr, s * lanes + j, cwg_ref, cbg_ref)
            val = conv(uv_scr, s * lanes + j, cwv_ref, cbv_ref)
            zs.append((gate * _sigmoid(gate) * val).astype(BF16))
        acc_scr[...] += _dot(jnp.concatenate(zs, axis=1), wd_ref[s * FFN_SUB:(s + 1) * FFN_SUB, :])

    up(0)
    for s in range(1, n_sub):
        up(s)
        down(s - 1)
    down(n_sub - 1)

    @pl.when(f == pl.num_programs(2) - 1)
    def _():
        for j in range(acc_scr.shape[1] // 128):
            a = acc_scr[:, j * 128:(j + 1) * 128]
            y_scr[j, pl.ds(0, tm // 2, stride=2), :] = a[:tm // 2]
            y_scr[j, pl.ds(1, tm // 2, stride=2), :] = a[tm // 2:]
        ffn = jnp.concatenate([y_scr[j] for j in range(y_scr.shape[0])], axis=1)
        y = x_ref[...] + mod_ref[5:6, :] * ffn
        if final_norm:
            ms = jnp.mean(y * y, axis=-1, keepdims=True)
            y = y * lax.rsqrt(ms + NORM_EPS) * fg_ref[...]
        o_ref[...] = y


def _ffn_call(x, mod_l, g, w_up, dw_w, dw_b, w_down, layer_idx, is_ctx, final_g=None):
    b_, s_, d = x.shape
    hid = w_down.shape[1]
    tm = min(512, s_)
    tf = 512
    n_f = hid // tf
    nh = tm // FFN_HALO
    last_halo = s_ // FFN_HALO - 1
    mod_row = (lambda b: b_) if is_ctx else (lambda b: b)
    return pl.pallas_call(
        functools.partial(_ffn_kernel, tm=tm, final_norm=final_g is not None),
        grid=(b_, s_ // tm, n_f),
        in_specs=[pl.BlockSpec((None, tm, d), lambda b, i, f: (b, i, 0)),
                  pl.BlockSpec((None, FFN_HALO, d), lambda b, i, f: (b, jnp.maximum(i * nh - 1, 0), 0)),
                  pl.BlockSpec((None, FFN_HALO, d), lambda b, i, f: (b, jnp.minimum((i + 1) * nh, last_halo), 0)),
                  pl.BlockSpec((None, 6, d), lambda b, i, f: (mod_row(b), 0, 0)),
                  pl.BlockSpec((1, d), lambda b, i, f: (0, 0)),
                  pl.BlockSpec((1, d), lambda b, i, f: (0, 0)),
                  pl.BlockSpec((None, d, tf), lambda b, i, f: (layer_idx, 0, f)),
                  pl.BlockSpec((None, d, tf), lambda b, i, f: (layer_idx, 0, n_f + f)),
                  pl.BlockSpec((None, 3, tf), lambda b, i, f: (layer_idx, 0, f)),
                  pl.BlockSpec((None, 3, tf), lambda b, i, f: (layer_idx, 0, n_f + f)),
                  pl.BlockSpec((None, 1, tf), lambda b, i, f: (layer_idx, 0, f)),
                  pl.BlockSpec((None, 1, tf), lambda b, i, f: (layer_idx, 0, n_f + f)),
                  pl.BlockSpec((None, tf, d), lambda b, i, f: (layer_idx, f, 0))],
        out_specs=pl.BlockSpec((None, tm, d), lambda b, i, f: (b, i, 0)),
        out_shape=jax.ShapeDtypeStruct(x.shape, F32),
        scratch_shapes=[pltpu.VMEM((tm + 2 * FFN_HALO, d), BF16), pltpu.VMEM((tm, d), F32),
                        pltpu.VMEM((tf // 128, tm + 2 * FFN_HALO, 128), F32),
                        pltpu.VMEM((tf // 128, tm + 2 * FFN_HALO, 128), F32),
                        pltpu.VMEM((d // 128, tm, 128), F32)],
        compiler_params=_params("arbitrary", "arbitrary", "arbitrary"),
        name="ffn_ctx" if is_ctx else "ffn_lat",
    )(x, x, x, mod_l, g, g if final_g is None else final_g, w_up, w_up, dw_w, dw_w, dw_b, dw_b, w_down)


def _rows_of_step(t, part):
    return pl.ds(t * S5_TPITCH + part, S5_BLOCKS, stride=2)


def _rows_of_block(s, part):
    return pl.ds(2 * s + part, S5_CHUNK, stride=S5_TPITCH)


def _s5_kernel(ucf_ref, ulf_ref, ucb_ref, ulb_ref, wb_ref, wc_ref, lam_ref,
               ycf_ref, ylf_ref, ycb_ref, ylb_ref, bu_f, bu_b, st_f, st_b, h_scr, *, n_cc):
    j = pl.program_id(1)
    bu, st = (bu_f, bu_b), (st_f, st_b)

    @pl.when(j == 0)
    def _():
        h_scr[...] = jnp.zeros_like(h_scr)

    def drive(u_ref, r):
        for c in range(8):
            res = _dot(u_ref[:, c * 128:(c + 1) * 128], wb_ref[r, c])
            for sl in range(4):
                for part in range(2):
                    col = sl * 256 + part * 128
                    bu[r][_rows_of_block(4 * c + sl, part), :] = res[:, col:col + 128]

    @pl.when(j < n_cc)
    def _():
        drive(ucf_ref, 0)
        drive(ucb_ref, 1)

    @pl.when(j >= n_cc)
    def _():
        drive(ulf_ref, 0)
        drive(ulb_ref, 1)

    lam = [lam_ref[k] for k in range(4)]

    def step(carry, t, r):
        hr, hi = carry
        lr, li = lam[2 * r], lam[2 * r + 1]
        nr = lr * hr - li * hi + bu[r][_rows_of_step(t, 0), :]
        ni = lr * hi + li * hr + bu[r][_rows_of_step(t, 1), :]
        st[r][_rows_of_step(t, 0), :] = nr
        st[r][_rows_of_step(t, 1), :] = ni
        return nr, ni

    def body(it, carry):
        cf, cb = carry[:2], carry[2:]
        for jj in range(S5_UNROLL):
            t = it * S5_UNROLL + jj
            cf = step(cf, t, 0)
            cb = step(cb, S5_CHUNK - 1 - t, 1)
        return cf + cb

    init = tuple(h_scr[k] for k in range(4))
    fin = lax.fori_loop(0, S5_CHUNK // S5_UNROLL, body, init)
    for k in range(4):
        h_scr[k] = fin[k]

    def readout(y_ref, r):
        for c in range(8):
            parts = [st[r][_rows_of_block(4 * c + sl, part), :].astype(BF16)
                     for sl in range(4) for part in range(2)]
            y_ref[:, c * 128:(c + 1) * 128] = _dot(jnp.concatenate(parts, axis=1), wc_ref[r, c])

    @pl.when(j < n_cc)
    def _():
        readout(ycf_ref, 0)
        readout(ycb_ref, 1)

    @pl.when(j >= n_cc)
    def _():
        readout(ylf_ref, 0)
        readout(ylb_ref, 1)


def _s5_call(z_ctx, z_lat, wb, wc, lam):
    b_, c_len, _ = z_ctx.shape
    l_len = z_lat.shape[1]
    tc = S5_CHUNK
    n_cc, n_lc = c_len // tc, l_len // tc
    ch = HALF_MIX

    def cf(b, j): return (b, jnp.minimum(j, n_cc - 1), 0)
    def lf(b, j): return (b, jnp.maximum(j - n_cc, 0), 0)
    def cb(b, j): return (b, jnp.maximum(n_cc - 1 - j, 0), 0)
    def lb(b, j): return (b, jnp.minimum(n_lc - 1 - (j - n_cc), n_lc - 1), 0)

    def spec(m): return pl.BlockSpec((None, tc, ch), m)

    return pl.pallas_call(
        functools.partial(_s5_kernel, n_cc=n_cc),
        grid=(b_, n_cc + n_lc),
        in_specs=[spec(cf), spec(lf), spec(cb), spec(lb),
                  _resident(wb.shape, lambda b, j: (0, 0, 0, 0)),
                  _resident(wc.shape, lambda b, j: (0, 0, 0, 0)),
                  _resident(lam.shape, lambda b, j: (0, 0, 0))],
        out_specs=[spec(cf), spec(lf), spec(cb), spec(lb)],
        out_shape=[jax.ShapeDtypeStruct((b_, c_len, ch), F32), jax.ShapeDtypeStruct((b_, l_len, ch), F32),
                   jax.ShapeDtypeStruct((b_, c_len, ch), F32), jax.ShapeDtypeStruct((b_, l_len, ch), F32)],
        scratch_shapes=[pltpu.VMEM((S5_CHUNK * S5_TPITCH, 128), F32)] * 4 + [pltpu.VMEM((4, S5_BLOCKS, 128), F32)],
        compiler_params=_params("arbitrary", "arbitrary"),
        name="s5_scan",
    )(z_ctx, z_lat, z_ctx, z_lat, wb, wc, lam)


def _s5_pack(lam_re, lam_im, log_dt, b_re, b_im, c_re, c_im):
    lam_re = jnp.minimum(lam_re.astype(F32), -1e-4)
    lam_im = lam_im.astype(F32)
    dt = jnp.exp(log_dt.astype(F32))[..., None]
    mag = jnp.exp(lam_re * dt)
    lb_re, lb_im = mag * jnp.cos(lam_im * dt), mag * jnp.sin(lam_im * dt)
    den = lam_re * lam_re + lam_im * lam_im
    num_re = lb_re - 1.0
    coef_re = (num_re * lam_re + lb_im * lam_im) / den
    coef_im = (lb_im * lam_re - num_re * lam_im) / den
    b_re, b_im = b_re.astype(F32), b_im.astype(F32)
    bb_re = coef_re[..., None] * b_re - coef_im[..., None] * b_im
    bb_im = coef_re[..., None] * b_im + coef_im[..., None] * b_re
    lam = jnp.stack([lb_re[0], lb_im[0], lb_re[1], lb_im[1]]).reshape(4, S5_BLOCKS, 128)
    eye4, eye2 = jnp.eye(4, dtype=F32), jnp.eye(2, dtype=F32)
    bb = jnp.stack([bb_re, bb_im], axis=1).reshape(2, 2, 8, 4, 2, S5_STATE, S5_GROUP_CH)
    wb = jnp.einsum('rxcsgph,ts,kg->rctkhsxgp', bb, eye4, eye2).reshape(2, 8, 128, 1024)
    cc = jnp.stack([c_re.astype(F32), -c_im.astype(F32)], axis=1).reshape(2, 2, 8, 4, 2, S5_GROUP_CH, S5_STATE)
    wc = jnp.einsum('rxcsghp,ts,kg->rcsxgptkh', cc, eye4, eye2).reshape(2, 8, 1024, 128)
    return wb.astype(BF16), wc.astype(BF16), lam


def _glu_kernel(yf_ref, yb_ref, u_ref, d_ref, w_ref, o_ref):
    y = yf_ref[...] + yb_ref[...] + d_ref[...] * u_ref[...].astype(F32)
    t = 0.5 * y * (1.0 + jnp.tanh(math.sqrt(2.0 / math.pi) * (y + 0.044715 * (y * y * y))))
    gate = _dot(t.astype(BF16), w_ref[...])
    o_ref[...] = (t * _sigmoid(gate)).astype(BF16)


def _glu_call(yf, yb, z, d_skip, w_all, layer_idx):
    b_, s_, ch = yf.shape
    tm = 256
    return pl.pallas_call(
        _glu_kernel,
        grid=(b_, s_ // tm),
        in_specs=[pl.BlockSpec((None, tm, ch), lambda b, i: (b, i, 0)),
                  pl.BlockSpec((None, tm, ch), lambda b, i: (b, i, 0)),
                  pl.BlockSpec((None, tm, ch), lambda b, i: (b, i, 0)),
                  pl.BlockSpec((1, ch), lambda b, i: (0, 0)),
                  _resident((None, ch, ch), lambda b, i: (layer_idx, 0, 0))],
        out_specs=pl.BlockSpec((None, tm, ch), lambda b, i: (b, i, 0)),
        out_shape=jax.ShapeDtypeStruct((b_, s_, ch), BF16),
        compiler_params=_params("arbitrary", "arbitrary"),
        name="s5_glu",
    )(yf, yb, z, d_skip, w_all)


def _logits(q, segs):
    logits = []
    for k, _, bias, mask in segs:
        s = _dot_nt(q, k) * LOGIT_SCALE
        if bias is not None:
            s = s + bias
        if mask is not None:
            s = jnp.where(mask, s, NEG_INF)
        logits.append(s)
    return logits


LOGIT_SCALE = HEAD_DIM ** -0.5 * LOG2_E


def _softmax_pv(logits, values, sink=None, raw=False):
    m = jnp.max(functools.reduce(jnp.maximum, [_fold_lanes(s(), jnp.maximum) for s in logits]),
                axis=-1, keepdims=True)
    if sink is not None:
        m = jnp.maximum(m, sink)
    psum = None
    out = 0.0
    for s, v in zip(logits, values):
        p = jnp.exp2((s() - m) * LOGIT_SCALE) if raw else jnp.exp2(s() - m)
        folded = _fold_lanes(p, jnp.add)
        psum = folded if psum is None else psum + folded
        out = out + _dot(p.astype(BF16), v())
    denom = jnp.sum(psum, axis=-1, keepdims=True)
    if sink is not None:
        denom = denom + jnp.exp2(sink - m)
    return out / denom


def _attend(q, segs, sink=None):
    return _softmax_pv([functools.partial(lambda s: s, s) for s in _logits(q, segs)],
                       [functools.partial(lambda v: v, v) for _, v, _, _ in segs], sink)


def _attend_pipelined(units, s_scr):
    def stage(u):
        q, segs, sink, emit = units[u]()
        raw = sink is None and all(bias is None and mask is None for _, _, bias, mask in segs)
        col = 0
        for s in ([_dot_nt(q, k) for k, _, _, _ in segs] if raw else _logits(q, segs)):
            s_scr[u % 2, :, col:col + s.shape[1]] = s
            col += s.shape[1]
        return segs, sink, emit, raw

    pending = stage(0)
    for u in range(len(units)):
        segs, sink, emit, raw = pending
        if u + 1 < len(units):
            pending = stage(u + 1)
        logits, values, col = [], [], 0
        for k, v, _, _ in segs:
            for j in range(0, k.shape[0], ATT_PIECE):
                hi = min(j + ATT_PIECE, k.shape[0])
                logits.append(functools.partial(lambda slot, a, b: s_scr[slot, :, a:b], u % 2, col + j, col + hi))
                values.append(functools.partial(_rows, v, slice(j, hi)))
            col += k.shape[0]
        emit(_softmax_pv(logits, values, sink, raw))


ATT_PIECE = 256


def _rows(v, rows):
    return v(rows) if callable(v) else v[rows]


def _fold_lanes(x, op):
    return functools.reduce(op, [x[:, i:i + 128] for i in range(0, x.shape[1], 128)])


def _head(ref, h, width=HEAD_DIM):
    return ref[:, h * width:(h + 1) * width]


def _stack_heads(q_ref, kh, group):
    return jnp.concatenate([_head(q_ref, kh * group + g) for g in range(group)], axis=0)


def _sink_column(sink_ref, kh, group, rows):
    return jnp.concatenate([jnp.full((rows, 1), sink_ref[kh * group + g] * LOG2_E, F32) for g in range(group)],
                           axis=0)


def _unstack_heads(o_ref, o, kh, group, rows):
    for g in range(group):
        h = kh * group + g
        o_ref[:, h * HEAD_DIM:(h + 1) * HEAD_DIM] = o[g * rows:(g + 1) * rows].astype(o_ref.dtype)


def _dense_attn_kernel(q_ref, k_ref, v_ref, *rest, kv_heads, group, has_sink):
    if has_sink:
        sink_ref, o_ref = rest
    else:
        (o_ref,) = rest
    rows = q_ref.shape[0]
    for kh in range(kv_heads):
        q = _stack_heads(q_ref, kh, group)
        sink = _sink_column(sink_ref, kh, group, rows) if has_sink else None
        o = _attend(q, [(_head(k_ref, kh), _head(v_ref, kh), None, None)], sink)
        _unstack_heads(o_ref, o, kh, group, rows)


def _dense_ctx_attn_call(z_ctx, q_col, k_col, v_col, kv_heads, group, sink=None):
    b_, c_len, _ = z_ctx.shape
    qw, kw = kv_heads * group * HEAD_DIM, kv_heads * HEAD_DIM
    in_specs = [pl.BlockSpec((None, c_len, qw), lambda b: (b, 0, q_col)),
                pl.BlockSpec((None, c_len, kw), lambda b: (b, 0, k_col)),
                pl.BlockSpec((None, c_len, kw), lambda b: (b, 0, v_col))]
    args = [z_ctx, z_ctx, z_ctx]
    if sink is not None:
        in_specs.append(pl.BlockSpec(memory_space=pltpu.SMEM))
        args.append(sink)
    return pl.pallas_call(
        functools.partial(_dense_attn_kernel, kv_heads=kv_heads, group=group, has_sink=sink is not None),
        grid=(b_,),
        in_specs=in_specs,
        out_specs=pl.BlockSpec((None, c_len, qw), lambda b: (b, 0, 0)),
        out_shape=jax.ShapeDtypeStruct((b_, c_len, qw), BF16),
        compiler_params=_params("arbitrary"),
        name="ctx_attn",
    )(*args)


def _na_kernel(q_ref, k0_ref, k1_ref, k2_ref, v0_ref, v1_ref, v2_ref, kc_ref, vc_ref, bias_ref, o_ref, s_scr):
    kb = k0_ref.shape[0]

    def unit(h):
        def emit(o):
            o_ref[:, h * HEAD_DIM:(h + 1) * HEAD_DIM] = o.astype(o_ref.dtype)

        segs = [(_head(k_ref, h), _head(v_ref, h), bias_ref[h, :, t * kb:(t + 1) * kb], None)
                for t, (k_ref, v_ref) in enumerate(((k0_ref, v0_ref), (k1_ref, v1_ref), (k2_ref, v2_ref)))]
        segs.append((_head(kc_ref, h), _head(vc_ref, h), None, None))
        return _head(q_ref, h), segs, None, emit

    _attend_pipelined([functools.partial(unit, h) for h in range(HALF_MIX // HEAD_DIM)], s_scr)


def _na_bias_tables(rpb, rows):
    tables = []
    a = np.arange(NA_QROWS)[:, None]
    c = np.arange(NA_KROWS)[None, :]
    qc = np.arange(GRID_W)[:, None]
    kc = np.arange(GRID_W)[None, :]
    ws = np.clip(qc - NA_COLS // 2, 0, GRID_W - NA_COLS)
    col_ok = (kc >= ws) & (kc < ws + NA_COLS)
    dc = np.clip(kc - qc + NA_COLS - 1, 0, 2 * NA_COLS - 2)
    onehot = (np.arange(2 * NA_COLS - 1)[:, None, None] == dc[None]).astype(np.float32)
    by_col = jnp.einsum('hrj,jqk->hrqk', rpb.astype(F32), onehot, precision=lax.Precision.HIGHEST)
    for r0 in (0, NA_QROWS, rows - NA_QROWS):
        base = int(np.clip(r0 - NA_ROWS // 2, 0, rows - NA_KROWS))
        qrow, krow = r0 + a, base + c
        rs = np.clip(qrow - NA_ROWS // 2, 0, rows - NA_ROWS)
        row_ok = (krow >= rs) & (krow < rs + NA_ROWS)
        dr = np.clip(krow - qrow + NA_ROWS - 1, 0, 2 * NA_ROWS - 2)
        vals = by_col[:, dr]
        ok = row_ok[:, :, None, None] & col_ok[None, None, :, :]
        t = jnp.where(ok[None], vals * LOG2_E, NEG_INF).transpose(0, 1, 3, 2, 4)
        tables.append(t.reshape(rpb.shape[0], NA_QROWS * GRID_W, NA_KROWS * GRID_W))
    return jnp.stack(tables)


def _na_call(z_lat, z_ctx, bias):
    b_, l_len, _ = z_lat.shape
    c_len = z_ctx.shape[1]
    qb = NA_QROWS * GRID_W
    nblk = l_len // qb
    w = HALF_MIX

    def kmap(t, col):
        return lambda b, i: (b, jnp.clip(i - 1, 0, nblk - 3) + t, col)

    def variant(b, i):
        return (jnp.where(i == 0, 0, jnp.where(i == nblk - 1, 2, 1)), 0, 0, 0)

    return pl.pallas_call(
        _na_kernel,
        grid=(b_, nblk),
        in_specs=[pl.BlockSpec((None, qb, w), lambda b, i: (b, i, 1))]
                 + [pl.BlockSpec((None, qb, w), kmap(t, 2)) for t in range(3)]
                 + [pl.BlockSpec((None, qb, w), kmap(t, 3)) for t in range(3)]
                 + [pl.BlockSpec((None, c_len, w), lambda b, i: (b, 0, 2)),
                    pl.BlockSpec((None, c_len, w), lambda b, i: (b, 0, 3)),
                    pl.BlockSpec((None,) + bias.shape[1:], variant)],
        out_specs=pl.BlockSpec((None, qb, w), lambda b, i: (b, i, 0)),
        out_shape=jax.ShapeDtypeStruct((b_, l_len, w), BF16),
        scratch_shapes=[pltpu.VMEM((2, qb, NA_KROWS * GRID_W + c_len), F32)],
        compiler_params=_params("arbitrary", "arbitrary"),
        name="na_attn",
    )(z_lat, z_lat, z_lat, z_lat, z_lat, z_lat, z_lat, z_ctx, z_ctx, bias)


def _diff_kernel(q_ref, kc_ref, vc_ref, *rest, has_lat, out_scale):
    if has_lat:
        kl_ref, vl_ref, lam_ref, g_ref, o_ref, s_scr = rest
    else:
        lam_ref, g_ref, o_ref, s_scr = rest
    lam = lam_ref[0]
    first = {}

    def unit(h, c):
        def emit(o):
            if c == 0:
                first[h] = o
                return
            o = first.pop(h) - lam * o
            ms = jnp.mean(o * o, axis=-1, keepdims=True)
            y = o * lax.rsqrt(ms + NORM_EPS) * g_ref[...] * out_scale
            o_ref[:, h * 2 * HEAD_DIM:(h + 1) * 2 * HEAD_DIM] = y.astype(o_ref.dtype)

        vcols = slice(h * 2 * HEAD_DIM, (h + 1) * 2 * HEAD_DIM)
        segs = [(_head(kc_ref, 2 * h + c), lambda rows: vc_ref[rows, vcols], None, None)]
        if has_lat:
            segs.append((_head(kl_ref, 2 * h + c), lambda rows: vl_ref[rows, vcols], None, None))
        return _head(q_ref, 2 * h + c), segs, None, emit

    _attend_pipelined([functools.partial(unit, h, c) for h in range(HALF_MIX // (2 * HEAD_DIM)) for c in range(2)],
                      s_scr)


def _diff_call(z_q, z_ctx, z_lat, lam, subln_g, out_scale):
    b_, s_, _ = z_q.shape
    c_len = z_ctx.shape[1]
    tq = 256
    w = HALF_MIX
    in_specs = [pl.BlockSpec((None, tq, w), lambda b, i: (b, i, 0)),
                pl.BlockSpec((None, c_len, w), lambda b, i: (b, 0, 1)),
                pl.BlockSpec((None, c_len, w), lambda b, i: (b, 0, 2))]
    args = [z_q, z_ctx, z_ctx]
    if z_lat is not None:
        l_len = z_lat.shape[1]
        in_specs += [pl.BlockSpec((None, l_len, w), lambda b, i: (b, 0, 1)),
                     pl.BlockSpec((None, l_len, w), lambda b, i: (b, 0, 2))]
        args += [z_lat, z_lat]
    in_specs += [pl.BlockSpec(memory_space=pltpu.SMEM), pl.BlockSpec((1, 2 * HEAD_DIM), lambda b, i: (0, 0))]
    args += [lam, subln_g]
    n_keys = c_len + (0 if z_lat is None else z_lat.shape[1])
    return pl.pallas_call(
        functools.partial(_diff_kernel, has_lat=z_lat is not None, out_scale=out_scale),
        grid=(b_, s_ // tq),
        in_specs=in_specs,
        out_specs=pl.BlockSpec((None, tq, w), lambda b, i: (b, i, 0)),
        out_shape=jax.ShapeDtypeStruct((b_, s_, w), BF16),
        scratch_shapes=[pltpu.VMEM((2, tq, n_keys), F32)],
        compiler_params=_params("arbitrary", "arbitrary"),
        name="diff_attn_lat" if z_lat is not None else "diff_attn_ctx",
    )(*args)


def _swa_kernel(q_ref, kp_ref, kn0_ref, kn_ref, vp_ref, vn0_ref, vn_ref, kc_ref, vc_ref, sink_ref, o_ref, s_scr,
                *, kv_heads, group, stack):
    n = pl.program_id(1)
    nb = pl.num_programs(1)
    blk = SWA_BLOCK
    qi = lax.broadcasted_iota(jnp.int32, (stack * blk, blk), 0) % blk
    kj = lax.broadcasted_iota(jnp.int32, (stack * blk, blk), 1)
    prev_ok = (kj >= qi) & (n > 0)
    next_ok = (kj <= qi) & (n < nb - 1)

    def unit(kh, part):
        first = kh * (group // stack) + part
        segs = [(_head(kp_ref, kh), _head(vp_ref, kh), None, prev_ok),
                (_head(kn0_ref, kh), _head(vn0_ref, kh), None, None),
                (_head(kn_ref, kh), _head(vn_ref, kh), None, next_ok),
                (_head(kc_ref, kh), _head(vc_ref, kh), None, None)]
        return (_stack_heads(q_ref, first, stack), segs, _sink_column(sink_ref, first, stack, blk),
                lambda o: _unstack_heads(o_ref, o, first, stack, blk))

    _attend_pipelined([functools.partial(unit, kh, part) for kh in range(kv_heads) for part in range(group // stack)],
                      s_scr)


def _swa_call(z_lat, z_ctx, sink, kv_heads=2, group=4):
    b_, l_len, _ = z_lat.shape
    c_len = z_ctx.shape[1]
    assert SWA_WINDOW == SWA_BLOCK
    blk = SWA_BLOCK
    stack = 2
    nb = l_len // blk
    qw, kw = kv_heads * group * HEAD_DIM, kv_heads * HEAD_DIM
    q_col = 3 * HALF_MIX // qw
    k_col = 4 * HALF_MIX // kw
    v_col = k_col + 1

    def prev(col): return lambda b, n: (b, jnp.maximum(n - 1, 0), col)
    def cur(col): return lambda b, n: (b, n, col)
    def nxt(col): return lambda b, n: (b, jnp.minimum(n + 1, nb - 1), col)

    return pl.pallas_call(
        functools.partial(_swa_kernel, kv_heads=kv_heads, group=group, stack=stack),
        grid=(b_, nb),
        in_specs=[pl.BlockSpec((None, blk, qw), lambda b, n: (b, n, q_col))]
                 + [pl.BlockSpec((None, blk, kw), m(k_col)) for m in (prev, cur, nxt)]
                 + [pl.BlockSpec((None, blk, kw), m(v_col)) for m in (prev, cur, nxt)]
                 + [pl.BlockSpec((None, c_len, kw), lambda b, n: (b, 0, k_col)),
                    pl.BlockSpec((None, c_len, kw), lambda b, n: (b, 0, v_col)),
                    pl.BlockSpec(memory_space=pltpu.SMEM)],
        out_specs=pl.BlockSpec((None, blk, qw), lambda b, n: (b, n, 0)),
        out_shape=jax.ShapeDtypeStruct((b_, l_len, qw), BF16),
        scratch_shapes=[pltpu.VMEM((2, stack * blk, 3 * blk + c_len), F32)],
        compiler_params=_params("arbitrary", "arbitrary"),
        name="swa_attn",
    )(z_lat, z_lat, z_lat, z_lat, z_lat, z_lat, z_lat, z_ctx, z_ctx, sink)


def _rope_tables(l_len):
    t = jnp.arange(l_len, dtype=jnp.int32)
    pos = jnp.stack([t // GRID_W, t % GRID_W], axis=-1).astype(F32)
    inv = ROPE_BASE ** (-2.0 * jnp.arange(ROPE_FREQS, dtype=F32) / (2 * ROPE_FREQS))
    ang = pos[:, :, None] * inv
    cos, sin = jnp.cos(ang), jnp.sin(ang)
    cosf = jnp.concatenate([cos[:, 0], cos[:, 0], cos[:, 1], cos[:, 1]], axis=-1)
    sinf = jnp.concatenate([-sin[:, 0], sin[:, 0], -sin[:, 1], sin[:, 1]], axis=-1)
    return cosf, sinf


_OD_ROPE_HEADS = frozenset(list(range(0, 16)) + list(range(24, 34)))


def kernel(x, c, ctx, c_ctx, ada_w, ada_b, norm_mix_g, norm_ffn_g, w_out, ffn_w_up, ffn_dw_w, ffn_dw_b,
           ffn_w_down, ev_w_in, s5_lam_re, s5_lam_im, s5_log_dt, s5_b_re, s5_b_im, s5_c_re, s5_c_im, s5_d,
           s5_w_glu, na_rpb, od_w_in, diff_lambda, diff_subln_g, swa_sink, final_norm_g):
    b_, l_len, d = x.shape
    depth = ada_w.shape[0]
    assert d == D_MODEL and b_ < MOD_ROWS
    rows = l_len // GRID_W

    w_out_h = w_out.astype(BF16)
    w_up_h = ffn_w_up.astype(BF16)
    w_down_h = ffn_w_down.astype(BF16)
    ev_w_h = ev_w_in.astype(BF16)
    od_w_h = od_w_in.astype(BF16)
    glu_w_h = s5_w_glu.astype(BF16)
    dw_b = ffn_dw_b.reshape(depth, 1, -1)

    s_all = jnp.zeros((MOD_ROWS, d), F32).at[:b_].set(c).at[b_].set(c_ctx)
    mod = _ada_call(s_all, ada_w, ada_b).reshape(depth, MOD_ROWS, 6, d)
    rope = _rope_tables(l_len)

    xc = ctx
    for l in range(depth):
        with_ctx = l < depth - 1
        i = l // 2
        mod_l = mod[l]
        g_mix = norm_mix_g[l].reshape(1, d)
        g_ffn = norm_ffn_g[l].reshape(1, d)
        if l % 2 == 0:
            z_lat = _proj_call(x, mod_l, g_mix, ev_w_h, i, False)
            z_ctx = _proj_call(xc, mod_l, g_mix, ev_w_h, i, True)
            wb, wc, lam = _s5_pack(s5_lam_re[i], s5_lam_im[i], s5_log_dt[i], s5_b_re[i], s5_b_im[i],
                                   s5_c_re[i], s5_c_im[i])
            ycf, ylf, ycb, ylb = _s5_call(z_ctx, z_lat, wb, wc, lam)
            d_skip = s5_d[i].reshape(1, HALF_MIX)
            ya = _glu_call(ylf, ylb, z_lat, d_skip, glu_w_h, i)
            yb = _na_call(z_lat, z_ctx, _na_bias_tables(na_rpb[i], rows))
            if with_ctx:
                yac = _glu_call(ycf, ycb, z_ctx, d_skip, glu_w_h, i)
                ybc = _dense_ctx_attn_call(z_ctx, 1, 2, 3, HALF_MIX // HEAD_DIM, 1)
        else:
            lam_init = 0.8 - 0.6 * math.exp(-0.3 * l)
            z_lat = _proj_call(x, mod_l, g_mix, od_w_h, i, False, rope, _OD_ROPE_HEADS)
            z_ctx = _proj_call(xc, mod_l, g_mix, od_w_h, i, True)
            lf = diff_lambda[i].astype(F32)
            lam = (jnp.exp(jnp.sum(lf[0] * lf[1])) - jnp.exp(jnp.sum(lf[2] * lf[3])) + lam_init).reshape(1)
            subln = diff_subln_g[i].reshape(1, 2 * HEAD_DIM)
            sink = swa_sink[i].astype(F32)
            ya = _diff_call(z_lat, z_ctx, z_lat, lam, subln, 1.0 - lam_init)
            yb = _swa_call(z_lat, z_ctx, sink)
            if with_ctx:
                yac = _diff_call(z_ctx, z_ctx, None, lam, subln, 1.0 - lam_init)
                ybc = _dense_ctx_attn_call(z_ctx, 3, 16, 17, 2, 4, sink)
        x = _outproj_call(ya, yb, x, mod_l, w_out_h, l, False)
        x = _ffn_call(x, mod_l, g_ffn, w_up_h, ffn_dw_w, dw_b, w_down_h, l, False,
                      final_g=None if with_ctx else final_norm_g.reshape(1, d))
        if with_ctx:
            xc = _outproj_call(yac, ybc, xc, mod_l, w_out_h, l, True)
            xc = _ffn_call(xc, mod_l, g_ffn, w_up_h, ffn_dw_w, dw_b, w_down_h, l, True)
    return x
```

```python
import functools
import math

import jax
import jax.numpy as jnp
import numpy as np
from jax import lax
from jax.experimental import pallas as pl
from jax.experimental.pallas import tpu as pltpu

F32 = jnp.float32
BF16 = jnp.bfloat16

D_MODEL = 2048
HALF_MIX = D_MODEL // 2
FFN_HIDDEN = 5632
GRID_W = 64
HEAD_DIM = 128
NORM_EPS = 1e-6
NEG_INF = -1e30
ROPE_BASE = 10000.0
ROPE_FREQS = 32
LOG2_E = math.log2(math.e)

S5_GROUPS = 64
S5_GROUP_CH = 16
S5_STATE = 64
S5_BLOCKS = S5_GROUPS * S5_STATE // 128
S5_CHUNK = 128
S5_TPITCH = 2 * S5_BLOCKS + 4
S5_UNROLL = 8

NA_ROWS = 8
NA_COLS = 16
NA_QROWS = 4
NA_KROWS = 12

SWA_WINDOW = 128
SWA_BLOCK = 128

MOD_ROWS = 32
VMEM_LIMIT_V7X = 56 * 1024 * 1024


def _params(*sem):
    return pltpu.CompilerParams(dimension_semantics=sem, vmem_limit_bytes=VMEM_LIMIT_V7X)


def _resident(shape, index_map):
    return pl.BlockSpec(shape, index_map, pipeline_mode=pl.Buffered(1))


def _sigmoid(x):
    return 1.0 / (1.0 + jnp.exp(-x))


def _dot(a, b):
    return jnp.dot(a, b, preferred_element_type=F32)


def _dot_nt(a, b):
    return lax.dot_general(a, b, (((1,), (1,)), ((), ())), preferred_element_type=F32)


def _modnorm(x, g, shift, scale):
    ms = jnp.mean(x * x, axis=-1, keepdims=True)
    y = x * lax.rsqrt(ms + NORM_EPS) * g
    return y * (1.0 + scale) + shift


NORM_ROWS = 16
NORM_UNROLL = 8


def _modnorm_rows(src_ref, dst_ref, dst_row0, g_ref, mod_ref, shift_row, keep=None):
    def body(r, carry):
        rows = pl.ds(pl.multiple_of(r * NORM_ROWS, NORM_ROWS), NORM_ROWS)
        h = _modnorm(src_ref[rows, :], g_ref[...], mod_ref[shift_row:shift_row + 1, :],
                     mod_ref[shift_row + 1:shift_row + 2, :])
        if keep is not None:
            h = jnp.where(keep, h, 0.0)
        dst_ref[pl.ds(pl.multiple_of(dst_row0 + r * NORM_ROWS, NORM_ROWS), NORM_ROWS), :] = h.astype(BF16)
        return carry

    steps = src_ref.shape[0] // NORM_ROWS
    lax.fori_loop(0, steps, body, 0, unroll=math.gcd(steps, NORM_UNROLL))


def _ada_kernel(s_ref, w_ref, b_ref, o_ref):
    s = s_ref[...]
    s = s * _sigmoid(s)
    o_ref[...] = _dot(s.astype(BF16), w_ref[...].astype(BF16)) + b_ref[...]


def _ada_call(s_all, ada_w, ada_b):
    depth, d, n6 = ada_w.shape
    tn = 1024
    return pl.pallas_call(
        _ada_kernel,
        grid=(depth, n6 // tn),
        in_specs=[pl.BlockSpec((MOD_ROWS, d), lambda l, n: (0, 0)),
                  pl.BlockSpec((None, d, tn), lambda l, n: (l, 0, n)),
                  pl.BlockSpec((None, 1, tn), lambda l, n: (l, 0, n))],
        out_specs=pl.BlockSpec((None, MOD_ROWS, tn), lambda l, n: (l, 0, n)),
        out_shape=jax.ShapeDtypeStruct((depth, MOD_ROWS, n6), F32),
        compiler_params=_params("arbitrary", "arbitrary"),
        name="ada_mod",
    )(s_all, ada_w, ada_b.reshape(depth, 1, n6))


def _rope(a, cosf, sinf, first_half):
    fwd = pltpu.roll(a, HEAD_DIM - ROPE_FREQS, 1)
    bwd = pltpu.roll(a, ROPE_FREQS, 1)
    return a * cosf + jnp.where(first_half, fwd, bwd) * sinf


def _proj_kernel(x_ref, mod_ref, g_ref, w_ref, *rest, n_out, chunk, rope_heads):
    if rope_heads is not None:
        cos_ref, sin_ref, o_ref, h_scr = rest
    else:
        o_ref, h_scr = rest
    _modnorm_rows(x_ref, h_scr, 0, g_ref, mod_ref, 0)
    if rope_heads is not None:
        cosf, sinf = cos_ref[...], sin_ref[...]
        lane = lax.broadcasted_iota(jnp.int32, cosf.shape, 1)
        first_half = (lane % (2 * ROPE_FREQS)) < ROPE_FREQS
    for j in range(n_out // chunk):
        acc = _dot(h_scr[...], w_ref[:, j * chunk:(j + 1) * chunk])
        for hh in range(chunk // HEAD_DIM):
            head = j * (chunk // HEAD_DIM) + hh
            a = acc[:, hh * HEAD_DIM:(hh + 1) * HEAD_DIM]
            if rope_heads is not None and head in rope_heads:
                a = _rope(a, cosf, sinf, first_half)
            o_ref[:, head * HEAD_DIM:(head + 1) * HEAD_DIM] = a.astype(BF16)


def _proj_call(x, mod_l, g, w_all, layer_idx, is_ctx, rope=None, rope_heads=None):
    b_, s_, d = x.shape
    n_out = w_all.shape[2]
    tm = 256
    mod_row = (lambda b: b_) if is_ctx else (lambda b: b)
    in_specs = [pl.BlockSpec((None, tm, d), lambda b, i: (b, i, 0)),
                pl.BlockSpec((None, 6, d), lambda b, i: (mod_row(b), 0, 0)),
                pl.BlockSpec((1, d), lambda b, i: (0, 0)),
                _resident((None, d, n_out), lambda b, i: (layer_idx, 0, 0))]
    args = [x, mod_l, g, w_all]
    if rope is not None:
        in_specs += [pl.BlockSpec((tm, HEAD_DIM), lambda b, i: (i, 0)),
                     pl.BlockSpec((tm, HEAD_DIM), lambda b, i: (i, 0))]
        args += list(rope)
    else:
        rope_heads = None
    return pl.pallas_call(
        functools.partial(_proj_kernel, n_out=n_out, chunk=512, rope_heads=rope_heads),
        grid=(b_, s_ // tm),
        in_specs=in_specs,
        out_specs=pl.BlockSpec((None, tm, n_out), lambda b, i: (b, i, 0)),
        out_shape=jax.ShapeDtypeStruct((b_, s_, n_out), BF16),
        scratch_shapes=[pltpu.VMEM((tm, d), BF16)],
        compiler_params=_params("arbitrary", "arbitrary"),
        name="proj_ctx" if is_ctx else "proj_lat",
    )(*args)


def _outproj_kernel(ya_ref, yb_ref, x_ref, mod_ref, wa_ref, wb_ref, o_ref):
    acc = _dot(ya_ref[...], wa_ref[...]) + _dot(yb_ref[...], wb_ref[...])
    o_ref[...] = x_ref[...] + mod_ref[2:3, :] * acc


def _outproj_call(ya, yb, x, mod_l, w_all, layer_idx, is_ctx):
    b_, s_, d = x.shape
    tm = 256
    half = ya.shape[2]
    mod_row = (lambda b: b_) if is_ctx else (lambda b: b)
    return pl.pallas_call(
        _outproj_kernel,
        grid=(b_, s_ // tm),
        in_specs=[pl.BlockSpec((None, tm, half), lambda b, i: (b, i, 0)),
                  pl.BlockSpec((None, tm, half), lambda b, i: (b, i, 0)),
                  pl.BlockSpec((None, tm, d), lambda b, i: (b, i, 0)),
                  pl.BlockSpec((None, 6, d), lambda b, i: (mod_row(b), 0, 0)),
                  _resident((None, half, d), lambda b, i: (layer_idx, 0, 0)),
                  _resident((None, half, d), lambda b, i: (layer_idx, 1, 0))],
        out_specs=pl.BlockSpec((None, tm, d), lambda b, i: (b, i, 0)),
        out_shape=jax.ShapeDtypeStruct(x.shape, F32),
        compiler_params=_params("arbitrary", "arbitrary"),
        name="outproj_ctx" if is_ctx else "outproj_lat",
    )(ya, yb, x, mod_l, w_all, w_all)


FFN_HALO = 16
FFN_SUB = 256


def _ffn_kernel(x_ref, xp_ref, xn_ref, mod_ref, g_ref, fg_ref, wg_ref, wv_ref, cwg_ref, cwv_ref, cbg_ref, cbv_ref,
                wd_ref, o_ref, h_scr, acc_scr, ug_scr, uv_scr, y_scr, *, tm, final_norm):
    i = pl.program_id(1)
    f = pl.program_id(2)
    rows = tm + 2 * FFN_HALO

    @pl.when(f == 0)
    def _():
        _modnorm_rows(x_ref, h_scr, FFN_HALO, g_ref, mod_ref, 3)
        _modnorm_rows(xp_ref, h_scr, 0, g_ref, mod_ref, 3, keep=i > 0)
        _modnorm_rows(xn_ref, h_scr, FFN_HALO + tm, g_ref, mod_ref, 3, keep=i < pl.num_programs(1) - 1)
        acc_scr[...] = jnp.zeros_like(acc_scr)

    h = h_scr[...]
    n_sub = wd_ref.shape[0] // FFN_SUB

    lanes = FFN_SUB // 128

    def up(s):
        cols = slice(s * FFN_SUB, (s + 1) * FFN_SUB)
        for u_scr, w_ref in ((ug_scr, wg_ref), (uv_scr, wv_ref)):
            res = _dot(h, w_ref[:, cols])
            for j in range(lanes):
                u_scr[s * lanes + j] = res[:, j * 128:(j + 1) * 128]

    def conv(u_scr, slab, cw_ref, cb_ref):
        cols = slice(slab * 128, (slab + 1) * 128)
        before, even, odd, after = (u_scr[slab, pl.ds(FFN_HALO + k, tm // 2, stride=2), :] for k in (-1, 0, 1, 2))
        w0, w1, w2, b = cw_ref[0:1, cols], cw_ref[1:2, cols], cw_ref[2:3, cols], cb_ref[:, cols]
        return jnp.concatenate([w0 * before + w1 * even + w2 * odd + b,
                                w0 * even + w1 * odd + w2 * after + b], axis=0)

    def down(s):
        zs = []
        for j in range(lanes):
            gate = conv(ug_scr, s * lanes + j, cwg_ref, cbg_ref)
            val = conv(uv_scr, s * lanes + j, cwv_ref, cbv_ref)
            zs.append((gate * _sigmoid(gate) * val).astype(BF16))
        acc_scr[...] += _dot(jnp.concatenate(zs, axis=1), wd_ref[s * FFN_SUB:(s + 1) * FFN_SUB, :])

    up(0)
    for s in range(1, n_sub):
        up(s)
        down(s - 1)
    down(n_sub - 1)

    @pl.when(f == pl.num_programs(2) - 1)
    def _():
        for j in range(acc_scr.shape[1] // 128):
            a = acc_scr[:, j * 128:(j + 1) * 128]
            y_scr[j, pl.ds(0, tm // 2, stride=2), :] = a[:tm // 2]
            y_scr[j, pl.ds(1, tm // 2, stride=2), :] = a[tm // 2:]
        ffn = jnp.concatenate([y_scr[j] for j in range(y_scr.shape[0])], axis=1)
        y = x_ref[...] + mod_ref[5:6, :] * ffn
        if final_norm:
            ms = jnp.mean(y * y, axis=-1, keepdims=True)
            y = y * lax.rsqrt(ms + NORM_EPS) * fg_ref[...]
        o_ref[...] = y


def _ffn_call(x, mod_l, g, w_up, dw_w, dw_b, w_down, layer_idx, is_ctx, final_g=None):
    b_, s_, d = x.shape
    hid = w_down.shape[1]
    tm = min(512, s_)
    tf = 512
    n_f = hid // tf
    nh = tm // FFN_HALO
    last_halo = s_ // FFN_HALO - 1
    mod_row = (lambda b: b_) if is_ctx else (lambda b: b)
    return pl.pallas_call(
        functools.partial(_ffn_kernel, tm=tm, final_norm=final_g is not None),
        grid=(b_, s_ // tm, n_f),
        in_specs=[pl.BlockSpec((None, tm, d), lambda b, i, f: (b, i, 0)),
                  pl.BlockSpec((None, FFN_HALO, d), lambda b, i, f: (b, jnp.maximum(i * nh - 1, 0), 0)),
                  pl.BlockSpec((None, FFN_HALO, d), lambda b, i, f: (b, jnp.minimum((i + 1) * nh, last_halo), 0)),
                  pl.BlockSpec((None, 6, d), lambda b, i, f: (mod_row(b), 0, 0)),
                  pl.BlockSpec((1, d), lambda b, i, f: (0, 0)),
                  pl.BlockSpec((1, d), lambda b, i, f: (0, 0)),
                  pl.BlockSpec((None, d, tf), lambda b, i, f: (layer_idx, 0, f)),
                  pl.BlockSpec((None, d, tf), lambda b, i, f: (layer_idx, 0, n_f + f)),
                  pl.BlockSpec((None, 3, tf), lambda b, i, f: (layer_idx, 0, f)),
                  pl.BlockSpec((None, 3, tf), lambda b, i, f: (layer_idx, 0, n_f + f)),
                  pl.BlockSpec((None, 1, tf), lambda b, i, f: (layer_idx, 0, f)),
                  pl.BlockSpec((None, 1, tf), lambda b, i, f: (layer_idx, 0, n_f + f)),
                  pl.BlockSpec((None, tf, d), lambda b, i, f: (layer_idx, f, 0))],
        out_specs=pl.BlockSpec((None, tm, d), lambda b, i, f: (b, i, 0)),
        out_shape=jax.ShapeDtypeStruct(x.shape, F32),
        scratch_shapes=[pltpu.VMEM((tm + 2 * FFN_HALO, d), BF16), pltpu.VMEM((tm, d), F32),
                        pltpu.VMEM((tf // 128, tm + 2 * FFN_HALO, 128), F32),
                        pltpu.VMEM((tf // 128, tm + 2 * FFN_HALO, 128), F32),
                        pltpu.VMEM((d // 128, tm, 128), F32)],
        compiler_params=_params("arbitrary", "arbitrary", "arbitrary"),
        name="ffn_ctx" if is_ctx else "ffn_lat",
    )(x, x, x, mod_l, g, g if final_g is None else final_g, w_up, w_up, dw_w, dw_w, dw_b, dw_b, w_down)


def _rows_of_step(t, part):
    return pl.ds(t * S5_TPITCH + part, S5_BLOCKS, stride=2)


def _rows_of_block(s, part):
    return pl.ds(2 * s + part, S5_CHUNK, stride=S5_TPITCH)


def _s5_kernel(uf_ref, ub_ref, wb_ref, wc_ref, lam_ref, yf_ref, yb_ref, *scratch):
    s = pl.program_id(1)
    u, y = (uf_ref, ub_ref), (yf_ref, yb_ref)
    scr, h_scr = scratch[:8], scratch[8]
    bu, st = (scr[0:2], scr[2:4]), (scr[4:6], scr[6:8])

    @pl.when((s == 0) & (pl.program_id(0) == 0))
    def _():
        for ref in scr + (h_scr,):
            ref[...] = jnp.zeros_like(ref)

    def run(drv):
        scn = 1 - drv

        def drive(r, c):
            res = _dot(u[r][:, c * 128:(c + 1) * 128], wb_ref[r, c])
            for sl in range(4):
                for part in range(2):
                    col = sl * 256 + part * 128
                    bu[r][drv][_rows_of_block(4 * c + sl, part), :] = res[:, col:col + 128]

        def step(carry, t, r):
            hr, hi = carry
            lr, li = lam_ref[2 * r], lam_ref[2 * r + 1]
            nr = lr * hr - li * hi + bu[r][scn][_rows_of_step(t, 0), :]
            ni = lr * hi + li * hr + bu[r][scn][_rows_of_step(t, 1), :]
            st[r][scn][_rows_of_step(t, 0), :] = nr
            st[r][scn][_rows_of_step(t, 1), :] = ni
            return nr, ni

        def readout(r, c):
            parts = [st[r][drv][_rows_of_block(4 * c + sl, part), :].astype(BF16)
                     for sl in range(4) for part in range(2)]
            y[r][:, c * 128:(c + 1) * 128] = _dot(jnp.concatenate(parts, axis=1), wc_ref[r, c])

        carry = [tuple(jnp.where(s == 1, 0.0, h_scr[2 * r + part]) for part in range(2)) for r in range(2)]
        per_slice = S5_CHUNK // 16
        for it in range(16):
            r, c = divmod(it, 8)
            drive(r, c)
            for t in range(it * per_slice, (it + 1) * per_slice):
                carry[0] = step(carry[0], t, 0)
                carry[1] = step(carry[1], S5_CHUNK - 1 - t, 1)
            readout(r, c)
        for r in range(2):
            for part in range(2):
                h_scr[2 * r + part] = carry[r][part]

    for drv in range(2):
        pl.when(s % 2 == drv)(functools.partial(run, drv))


def _s5_call(u_all, n_ctx_chunks, wb, wc, lam):
    b_, t_len, ch = u_all.shape
    tc = S5_CHUNK
    n = t_len // tc

    def fwd(p):
        return p

    def bwd(p):
        return jnp.where(p < n_ctx_chunks, n_ctx_chunks - 1 - p, n + n_ctx_chunks - 1 - p)

    def spec(order, lag):
        return pl.BlockSpec((None, tc, ch), lambda b, s: (b, order(jnp.clip(s - lag, 0, n - 1)), 0))

    return pl.pallas_call(
        _s5_kernel,
        grid=(b_, n + 2),
        in_specs=[spec(fwd, 0), spec(bwd, 0),
                  _resident(wb.shape, lambda b, s: (0, 0, 0, 0)),
                  _resident(wc.shape, lambda b, s: (0, 0, 0, 0)),
                  _resident(lam.shape, lambda b, s: (0, 0, 0))],
        out_specs=[spec(fwd, 2), spec(bwd, 2)],
        out_shape=[jax.ShapeDtypeStruct((b_, t_len, ch), F32)] * 2,
        scratch_shapes=[pltpu.VMEM((S5_CHUNK * S5_TPITCH, 128), F32)] * 8
                       + [pltpu.VMEM((4, S5_BLOCKS, 128), F32)],
        compiler_params=_params("arbitrary", "arbitrary"),
        name="s5_scan",
    )(u_all, u_all, wb, wc, lam)


def _s5_pack(lam_re, lam_im, log_dt, b_re, b_im, c_re, c_im):
    lam_re = jnp.minimum(lam_re.astype(F32), -1e-4)
    lam_im = lam_im.astype(F32)
    dt = jnp.exp(log_dt.astype(F32))[..., None]
    mag = jnp.exp(lam_re * dt)
    lb_re, lb_im = mag * jnp.cos(lam_im * dt), mag * jnp.sin(lam_im * dt)
    den = lam_re * lam_re + lam_im * lam_im
    num_re = lb_re - 1.0
    coef_re = (num_re * lam_re + lb_im * lam_im) / den
    coef_im = (lb_im * lam_re - num_re * lam_im) / den
    b_re, b_im = b_re.astype(F32), b_im.astype(F32)
    bb_re = coef_re[..., None] * b_re - coef_im[..., None] * b_im
    bb_im = coef_re[..., None] * b_im + coef_im[..., None] * b_re
    lam = jnp.stack([lb_re[0], lb_im[0], lb_re[1], lb_im[1]]).reshape(4, S5_BLOCKS, 128)
    eye4, eye2 = jnp.eye(4, dtype=F32), jnp.eye(2, dtype=F32)
    bb = jnp.stack([bb_re, bb_im], axis=1).reshape(2, 2, 8, 4, 2, S5_STATE, S5_GROUP_CH)
    wb = jnp.einsum('rxcsgph,ts,kg->rctkhsxgp', bb, eye4, eye2).reshape(2, 8, 128, 1024)
    cc = jnp.stack([c_re.astype(F32), -c_im.astype(F32)], axis=1).reshape(2, 2, 8, 4, 2, S5_GROUP_CH, S5_STATE)
    wc = jnp.einsum('rxcsghp,ts,kg->rcsxgptkh', cc, eye4, eye2).reshape(2, 8, 1024, 128)
    return wb.astype(BF16), wc.astype(BF16), lam


def _glu_kernel(yf_ref, yb_ref, u_ref, d_ref, w_ref, o_ref):
    y = yf_ref[...] + yb_ref[...] + d_ref[...] * u_ref[...].astype(F32)
    t = 0.5 * y * (1.0 + jnp.tanh(math.sqrt(2.0 / math.pi) * (y + 0.044715 * (y * y * y))))
    gate = _dot(t.astype(BF16), w_ref[...])
    o_ref[...] = (t * _sigmoid(gate)).astype(BF16)


def _glu_call(yf, yb, z, d_skip, w_all, layer_idx):
    b_, s_, ch = yf.shape
    tm = 256
    return pl.pallas_call(
        _glu_kernel,
        grid=(b_, s_ // tm),
        in_specs=[pl.BlockSpec((None, tm, ch), lambda b, i: (b, i, 0)),
                  pl.BlockSpec((None, tm, ch), lambda b, i: (b, i, 0)),
                  pl.BlockSpec((None, tm, ch), lambda b, i: (b, i, 0)),
                  pl.BlockSpec((1, ch), lambda b, i: (0, 0)),
                  _resident((None, ch, ch), lambda b, i: (layer_idx, 0, 0))],
        out_specs=pl.BlockSpec((None, tm, ch), lambda b, i: (b, i, 0)),
        out_shape=jax.ShapeDtypeStruct((b_, s_, ch), BF16),
        compiler_params=_params("arbitrary", "arbitrary"),
        name="s5_glu",
    )(yf, yb, z, d_skip, w_all)


def _logits(q, segs):
    logits = []
    for k, _, bias, mask in segs:
        s = _dot_nt(q, k) * LOGIT_SCALE
        if bias is not None:
            s = s + bias
        if mask is not None:
            s = jnp.where(mask, s, NEG_INF)
        logits.append(s)
    return logits


LOGIT_SCALE = HEAD_DIM ** -0.5 * LOG2_E


def _softmax_pv(logits, values, sink=None, raw=False):
    m = _row_max(logits, sink)
    return _normalise(*_exp_pv(logits, values, m, raw), m, sink)


def _row_max(logits, sink):
    m = jnp.max(functools.reduce(jnp.maximum, [_fold_lanes(s(), jnp.maximum) for s in logits]),
                axis=-1, keepdims=True)
    return m if sink is None else jnp.maximum(m, sink)


def _exp_pv(logits, values, m, raw):
    psum = None
    out = 0.0
    for s, v in zip(logits, values):
        p = jnp.exp2((s() - m) * LOGIT_SCALE) if raw else jnp.exp2(s() - m)
        folded = _fold_lanes(p, jnp.add)
        psum = folded if psum is None else psum + folded
        out = out + _dot(p.astype(BF16), v())
    return out, psum


def _normalise(out, psum, m, sink):
    denom = jnp.sum(psum, axis=-1, keepdims=True)
    if sink is not None:
        denom = denom + jnp.exp2(sink - m)
    return out / denom


def _attend(q, segs, sink=None):
    return _softmax_pv([functools.partial(lambda s: s, s) for s in _logits(q, segs)],
                       [functools.partial(lambda v: v, v) for _, v, _, _ in segs], sink)


def _attend_pipelined(units, s_scr):
    slots = s_scr.shape[0]

    def stage(u):
        q, segs, sink, emit = units[u]()
        raw = sink is None and all(bias is None and mask is None for _, _, bias, mask in segs)
        col = 0
        for s in ([_dot_nt(q, k) for k, _, _, _ in segs] if raw else _logits(q, segs)):
            s_scr[u % slots, :, col:col + s.shape[1]] = s
            col += s.shape[1]
        logits, values, col = [], [], 0
        for k, v, _, _ in segs:
            for j in range(0, k.shape[0], ATT_PIECE):
                hi = min(j + ATT_PIECE, k.shape[0])
                logits.append(functools.partial(lambda slot, a, b: s_scr[slot, :, a:b], u % slots, col + j, col + hi))
                values.append(functools.partial(_rows, v, slice(j, hi)))
            col += k.shape[0]
        return logits, values, sink, emit, raw

    pending = stage(0)
    for u in range(len(units)):
        logits, values, sink, emit, raw = pending
        if u + 1 < len(units):
            pending = stage(u + 1)
        emit(_softmax_pv(logits, values, sink, raw))


ATT_PIECE = 256


def _rows(v, rows):
    return v(rows) if callable(v) else v[rows]


def _fold_lanes(x, op):
    return functools.reduce(op, [x[:, i:i + 128] for i in range(0, x.shape[1], 128)])


def _head(ref, h, width=HEAD_DIM):
    return ref[:, h * width:(h + 1) * width]


def _stack_heads(q_ref, kh, group):
    return jnp.concatenate([_head(q_ref, kh * group + g) for g in range(group)], axis=0)


def _sink_column(sink_ref, kh, group, rows):
    return jnp.concatenate([jnp.full((rows, 1), sink_ref[kh * group + g] * LOG2_E, F32) for g in range(group)],
                           axis=0)


def _unstack_heads(o_ref, o, kh, group, rows):
    for g in range(group):
        h = kh * group + g
        o_ref[:, h * HEAD_DIM:(h + 1) * HEAD_DIM] = o[g * rows:(g + 1) * rows].astype(o_ref.dtype)


def _dense_attn_kernel(q_ref, k_ref, v_ref, *rest, kv_heads, group, has_sink):
    if has_sink:
        sink_ref, o_ref = rest
    else:
        (o_ref,) = rest
    rows = q_ref.shape[0]
    for kh in range(kv_heads):
        q = _stack_heads(q_ref, kh, group)
        sink = _sink_column(sink_ref, kh, group, rows) if has_sink else None
        o = _attend(q, [(_head(k_ref, kh), _head(v_ref, kh), None, None)], sink)
        _unstack_heads(o_ref, o, kh, group, rows)


def _dense_ctx_attn_call(z_ctx, q_col, k_col, v_col, kv_heads, group, sink=None):
    b_, c_len, _ = z_ctx.shape
    qw, kw = kv_heads * group * HEAD_DIM, kv_heads * HEAD_DIM
    in_specs = [pl.BlockSpec((None, c_len, qw), lambda b: (b, 0, q_col)),
                pl.BlockSpec((None, c_len, kw), lambda b: (b, 0, k_col)),
                pl.BlockSpec((None, c_len, kw), lambda b: (b, 0, v_col))]
    args = [z_ctx, z_ctx, z_ctx]
    if sink is not None:
        in_specs.append(pl.BlockSpec(memory_space=pltpu.SMEM))
        args.append(sink)
    return pl.pallas_call(
        functools.partial(_dense_attn_kernel, kv_heads=kv_heads, group=group, has_sink=sink is not None),
        grid=(b_,),
        in_specs=in_specs,
        out_specs=pl.BlockSpec((None, c_len, qw), lambda b: (b, 0, 0)),
        out_shape=jax.ShapeDtypeStruct((b_, c_len, qw), BF16),
        compiler_params=_params("arbitrary"),
        name="ctx_attn",
    )(*args)


def _na_kernel(q_ref, k0_ref, k1_ref, k2_ref, v0_ref, v1_ref, v2_ref, kc_ref, vc_ref, bias_ref, o_ref, s_scr):
    kb = k0_ref.shape[0]

    def unit(h):
        def emit(o):
            o_ref[:, h * HEAD_DIM:(h + 1) * HEAD_DIM] = o.astype(o_ref.dtype)

        segs = [(_head(k_ref, h), _head(v_ref, h), bias_ref[h, :, t * kb:(t + 1) * kb], None)
                for t, (k_ref, v_ref) in enumerate(((k0_ref, v0_ref), (k1_ref, v1_ref), (k2_ref, v2_ref)))]
        segs.append((_head(kc_ref, h), _head(vc_ref, h), None, None))
        return _head(q_ref, h), segs, None, emit

    _attend_pipelined([functools.partial(unit, h) for h in range(HALF_MIX // HEAD_DIM)], s_scr)


def _na_bias_tables(rpb, rows):
    tables = []
    a = np.arange(NA_QROWS)[:, None]
    c = np.arange(NA_KROWS)[None, :]
    qc = np.arange(GRID_W)[:, None]
    kc = np.arange(GRID_W)[None, :]
    ws = np.clip(qc - NA_COLS // 2, 0, GRID_W - NA_COLS)
    col_ok = (kc >= ws) & (kc < ws + NA_COLS)
    dc = np.clip(kc - qc + NA_COLS - 1, 0, 2 * NA_COLS - 2)
    onehot = (np.arange(2 * NA_COLS - 1)[:, None, None] == dc[None]).astype(np.float32)
    by_col = jnp.einsum('hrj,jqk->hrqk', rpb.astype(F32), onehot, precision=lax.Precision.HIGHEST)
    for r0 in (0, NA_QROWS, rows - NA_QROWS):
        base = int(np.clip(r0 - NA_ROWS // 2, 0, rows - NA_KROWS))
        qrow, krow = r0 + a, base + c
        rs = np.clip(qrow - NA_ROWS // 2, 0, rows - NA_ROWS)
        row_ok = (krow >= rs) & (krow < rs + NA_ROWS)
        dr = np.clip(krow - qrow + NA_ROWS - 1, 0, 2 * NA_ROWS - 2)
        vals = by_col[:, dr]
        ok = row_ok[:, :, None, None] & col_ok[None, None, :, :]
        t = jnp.where(ok[None], vals * LOG2_E, NEG_INF).transpose(0, 1, 3, 2, 4)
        tables.append(t.reshape(rpb.shape[0], NA_QROWS * GRID_W, NA_KROWS * GRID_W))
    return jnp.stack(tables)


def _na_call(z_lat, z_ctx, bias):
    b_, l_len, _ = z_lat.shape
    c_len = z_ctx.shape[1]
    qb = NA_QROWS * GRID_W
    nblk = l_len // qb
    w = HALF_MIX

    def kmap(t, col):
        return lambda b, i: (b, jnp.clip(i - 1, 0, nblk - 3) + t, col)

    def variant(b, i):
        return (jnp.where(i == 0, 0, jnp.where(i == nblk - 1, 2, 1)), 0, 0, 0)

    return pl.pallas_call(
        _na_kernel,
        grid=(b_, nblk),
        in_specs=[pl.BlockSpec((None, qb, w), lambda b, i: (b, i, 1))]
                 + [pl.BlockSpec((None, qb, w), kmap(t, 2)) for t in range(3)]
                 + [pl.BlockSpec((None, qb, w), kmap(t, 3)) for t in range(3)]
                 + [pl.BlockSpec((None, c_len, w), lambda b, i: (b, 0, 2)),
                    pl.BlockSpec((None, c_len, w), lambda b, i: (b, 0, 3)),
                    pl.BlockSpec((None,) + bias.shape[1:], variant)],
        out_specs=pl.BlockSpec((None, qb, w), lambda b, i: (b, i, 0)),
        out_shape=jax.ShapeDtypeStruct((b_, l_len, w), BF16),
        scratch_shapes=[pltpu.VMEM((2, qb, NA_KROWS * GRID_W + c_len), F32)],
        compiler_params=_params("arbitrary", "arbitrary"),
        name="na_attn",
    )(z_lat, z_lat, z_lat, z_lat, z_lat, z_lat, z_lat, z_ctx, z_ctx, bias)


def _diff_kernel(q_ref, kc_ref, vc_ref, *rest, has_lat, out_scale):
    if has_lat:
        kl_ref, vl_ref, lam_ref, g_ref, o_ref, s_scr = rest
    else:
        lam_ref, g_ref, o_ref, s_scr = rest
    lam = lam_ref[0]
    first = {}

    def unit(h, c):
        def emit(o):
            if c == 0:
                first[h] = o
                return
            o = first.pop(h) - lam * o
            ms = jnp.mean(o * o, axis=-1, keepdims=True)
            y = o * lax.rsqrt(ms + NORM_EPS) * g_ref[...] * out_scale
            o_ref[:, h * 2 * HEAD_DIM:(h + 1) * 2 * HEAD_DIM] = y.astype(o_ref.dtype)

        vcols = slice(h * 2 * HEAD_DIM, (h + 1) * 2 * HEAD_DIM)
        segs = [(_head(kc_ref, 2 * h + c), lambda rows: vc_ref[rows, vcols], None, None)]
        if has_lat:
            segs.append((_head(kl_ref, 2 * h + c), lambda rows: vl_ref[rows, vcols], None, None))
        return _head(q_ref, 2 * h + c), segs, None, emit

    _attend_pipelined([functools.partial(unit, h, c) for h in range(HALF_MIX // (2 * HEAD_DIM)) for c in range(2)],
                      s_scr)


def _diff_call(z_q, z_ctx, z_lat, lam, subln_g, out_scale):
    b_, s_, _ = z_q.shape
    c_len = z_ctx.shape[1]
    tq = 256
    w = HALF_MIX
    in_specs = [pl.BlockSpec((None, tq, w), lambda b, i: (b, i, 0)),
                pl.BlockSpec((None, c_len, w), lambda b, i: (b, 0, 1)),
                pl.BlockSpec((None, c_len, w), lambda b, i: (b, 0, 2))]
    args = [z_q, z_ctx, z_ctx]
    if z_lat is not None:
        l_len = z_lat.shape[1]
        in_specs += [pl.BlockSpec((None, l_len, w), lambda b, i: (b, 0, 1)),
                     pl.BlockSpec((None, l_len, w), lambda b, i: (b, 0, 2))]
        args += [z_lat, z_lat]
    in_specs += [pl.BlockSpec(memory_space=pltpu.SMEM), pl.BlockSpec((1, 2 * HEAD_DIM), lambda b, i: (0, 0))]
    args += [lam, subln_g]
    n_keys = c_len + (0 if z_lat is None else z_lat.shape[1])
    return pl.pallas_call(
        functools.partial(_diff_kernel, has_lat=z_lat is not None, out_scale=out_scale),
        grid=(b_, s_ // tq),
        in_specs=in_specs,
        out_specs=pl.BlockSpec((None, tq, w), lambda b, i: (b, i, 0)),
        out_shape=jax.ShapeDtypeStruct((b_, s_, w), BF16),
        scratch_shapes=[pltpu.VMEM((2, tq, n_keys), F32)],
        compiler_params=_params("arbitrary", "arbitrary"),
        name="diff_attn_lat" if z_lat is not None else "diff_attn_ctx",
    )(*args)


def _swa_kernel(q_ref, kp_ref, kn0_ref, kn_ref, vp_ref, vn0_ref, vn_ref, kc_ref, vc_ref, sink_ref, o_ref, s_scr,
                *, kv_heads, group, stack):
    n = pl.program_id(1)
    nb = pl.num_programs(1)
    blk = SWA_BLOCK
    qi = lax.broadcasted_iota(jnp.int32, (stack * blk, blk), 0) % blk
    kj = lax.broadcasted_iota(jnp.int32, (stack * blk, blk), 1)
    prev_ok = (kj >= qi) & (n > 0)
    next_ok = (kj <= qi) & (n < nb - 1)

    def unit(kh, part):
        first = kh * (group // stack) + part
        segs = [(_head(kp_ref, kh), _head(vp_ref, kh), None, prev_ok),
                (_head(kn0_ref, kh), _head(vn0_ref, kh), None, None),
                (_head(kn_ref, kh), _head(vn_ref, kh), None, next_ok),
                (_head(kc_ref, kh), _head(vc_ref, kh), None, None)]
        return (_stack_heads(q_ref, first, stack), segs, _sink_column(sink_ref, first, stack, blk),
                lambda o: _unstack_heads(o_ref, o, first, stack, blk))

    _attend_pipelined([functools.partial(unit, kh, part) for kh in range(kv_heads) for part in range(group // stack)],
                      s_scr)


def _swa_call(z_lat, z_ctx, sink, kv_heads=2, group=4):
    b_, l_len, _ = z_lat.shape
    c_len = z_ctx.shape[1]
    assert SWA_WINDOW == SWA_BLOCK
    blk = SWA_BLOCK
    stack = 2
    nb = l_len // blk
    qw, kw = kv_heads * group * HEAD_DIM, kv_heads * HEAD_DIM
    q_col = 3 * HALF_MIX // qw
    k_col = 4 * HALF_MIX // kw
    v_col = k_col + 1

    def prev(col): return lambda b, n: (b, jnp.maximum(n - 1, 0), col)
    def cur(col): return lambda b, n: (b, n, col)
    def nxt(col): return lambda b, n: (b, jnp.minimum(n + 1, nb - 1), col)

    return pl.pallas_call(
        functools.partial(_swa_kernel, kv_heads=kv_heads, group=group, stack=stack),
        grid=(b_, nb),
        in_specs=[pl.BlockSpec((None, blk, qw), lambda b, n: (b, n, q_col))]
                 + [pl.BlockSpec((None, blk, kw), m(k_col)) for m in (prev, cur, nxt)]
                 + [pl.BlockSpec((None, blk, kw), m(v_col)) for m in (prev, cur, nxt)]
                 + [pl.BlockSpec((None, c_len, kw), lambda b, n: (b, 0, k_col)),
                    pl.BlockSpec((None, c_len, kw), lambda b, n: (b, 0, v_col)),
                    pl.BlockSpec(memory_space=pltpu.SMEM)],
        out_specs=pl.BlockSpec((None, blk, qw), lambda b, n: (b, n, 0)),
        out_shape=jax.ShapeDtypeStruct((b_, l_len, qw), BF16),
        scratch_shapes=[pltpu.VMEM((2, stack * blk, 3 * blk + c_len), F32)],
        compiler_params=_params("arbitrary", "arbitrary"),
        name="swa_attn",
    )(z_lat, z_lat, z_lat, z_lat, z_lat, z_lat, z_lat, z_ctx, z_ctx, sink)


def _rope_tables(l_len):
    t = jnp.arange(l_len, dtype=jnp.int32)
    pos = jnp.stack([t // GRID_W, t % GRID_W], axis=-1).astype(F32)
    inv = ROPE_BASE ** (-2.0 * jnp.arange(ROPE_FREQS, dtype=F32) / (2 * ROPE_FREQS))
    ang = pos[:, :, None] * inv
    cos, sin = jnp.cos(ang), jnp.sin(ang)
    cosf = jnp.concatenate([cos[:, 0], cos[:, 0], cos[:, 1], cos[:, 1]], axis=-1)
    sinf = jnp.concatenate([-sin[:, 0], sin[:, 0], -sin[:, 1], sin[:, 1]], axis=-1)
    return cosf, sinf


_OD_ROPE_HEADS = frozenset(list(range(0, 16)) + list(range(24, 34)))


def kernel(x, c, ctx, c_ctx, ada_w, ada_b, norm_mix_g, norm_ffn_g, w_out, ffn_w_up, ffn_dw_w, ffn_dw_b,
           ffn_w_down, ev_w_in, s5_lam_re, s5_lam_im, s5_log_dt, s5_b_re, s5_b_im, s5_c_re, s5_c_im, s5_d,
           s5_w_glu, na_rpb, od_w_in, diff_lambda, diff_subln_g, swa_sink, final_norm_g):
    b_, l_len, d = x.shape
    depth = ada_w.shape[0]
    assert d == D_MODEL and b_ < MOD_ROWS
    rows = l_len // GRID_W

    w_out_h = w_out.astype(BF16)
    w_up_h = ffn_w_up.astype(BF16)
    w_down_h = ffn_w_down.astype(BF16)
    ev_w_h = ev_w_in.astype(BF16)
    od_w_h = od_w_in.astype(BF16)
    glu_w_h = s5_w_glu.astype(BF16)
    dw_b = ffn_dw_b.reshape(depth, 1, -1)

    s_all = jnp.zeros((MOD_ROWS, d), F32).at[:b_].set(c).at[b_].set(c_ctx)
    mod = _ada_call(s_all, ada_w, ada_b).reshape(depth, MOD_ROWS, 6, d)
    rope = _rope_tables(l_len)

    xc = ctx
    for l in range(depth):
        with_ctx = l < depth - 1
        i = l // 2
        mod_l = mod[l]
        g_mix = norm_mix_g[l].reshape(1, d)
        g_ffn = norm_ffn_g[l].reshape(1, d)
        if l % 2 == 0:
            z_lat = _proj_call(x, mod_l, g_mix, ev_w_h, i, False)
            z_ctx = _proj_call(xc, mod_l, g_mix, ev_w_h, i, True)
            wb, wc, lam = _s5_pack(s5_lam_re[i], s5_lam_im[i], s5_log_dt[i], s5_b_re[i], s5_b_im[i],
                                   s5_c_re[i], s5_c_im[i])
            c_len = z_ctx.shape[1]
            u_all = jnp.concatenate([z_ctx[:, :, :HALF_MIX], z_lat[:, :, :HALF_MIX]], axis=1)
            yf, yb_scan = _s5_call(u_all, c_len // S5_CHUNK, wb, wc, lam)
            ya_all = _glu_call(yf, yb_scan, u_all, s5_d[i].reshape(1, HALF_MIX), glu_w_h, i)
            ya = ya_all[:, c_len:]
            yb = _na_call(z_lat, z_ctx, _na_bias_tables(na_rpb[i], rows))
            if with_ctx:
                yac = ya_all[:, :c_len]
                ybc = _dense_ctx_attn_call(z_ctx, 1, 2, 3, HALF_MIX // HEAD_DIM, 1)
        else:
            lam_init = 0.8 - 0.6 * math.exp(-0.3 * l)
            z_lat = _proj_call(x, mod_l, g_mix, od_w_h, i, False, rope, _OD_ROPE_HEADS)
            z_ctx = _proj_call(xc, mod_l, g_mix, od_w_h, i, True)
            lf = diff_lambda[i].astype(F32)
            lam = (jnp.exp(jnp.sum(lf[0] * lf[1])) - jnp.exp(jnp.sum(lf[2] * lf[3])) + lam_init).reshape(1)
            subln = diff_subln_g[i].reshape(1, 2 * HEAD_DIM)
            sink = swa_sink[i].astype(F32)
            ya = _diff_call(z_lat, z_ctx, z_lat, lam, subln, 1.0 - lam_init)
            yb = _swa_call(z_lat, z_ctx, sink)
            if with_ctx:
                yac = _diff_call(z_ctx, z_ctx, None, lam, subln, 1.0 - lam_init)
                ybc = _dense_ctx_attn_call(z_ctx, 3, 16, 17, 2, 4, sink)
        x = _outproj_call(ya, yb, x, mod_l, w_out_h, l, False)
        x = _ffn_call(x, mod_l, g_ffn, w_up_h, ffn_dw_w, dw_b, w_down_h, l, False,
                      final_g=None if with_ctx else final_norm_g.reshape(1, d))
        if with_ctx:
            xc = _outproj_call(yac, ybc, xc, mod_l, w_out_h, l, True)
            xc = _ffn_call(xc, mod_l, g_ffn, w_up_h, ffn_dw_w, dw_b, w_down_h, l, True)
    return x
```

```python
import functools
import math

import jax
import jax.numpy as jnp
import numpy as np
from jax import lax
from jax.experimental import pallas as pl
from jax.experimental.pallas import tpu as pltpu

F32 = jnp.float32
BF16 = jnp.bfloat16

D_MODEL = 2048
HALF_MIX = D_MODEL // 2
FFN_HIDDEN = 5632
GRID_W = 64
HEAD_DIM = 128
NORM_EPS = 1e-6
NEG_INF = -1e30
ROPE_BASE = 10000.0
ROPE_FREQS = 32
LOG2_E = math.log2(math.e)

S5_GROUPS = 64
S5_GROUP_CH = 16
S5_STATE = 64
S5_BLOCKS = S5_GROUPS * S5_STATE // 128
S5_CHUNK = 128
S5_TPITCH = 2 * S5_BLOCKS + 4

NA_ROWS = 8
NA_COLS = 16
NA_QROWS = 4
NA_KROWS = 12

SWA_WINDOW = 128
SWA_BLOCK = 128

MOD_ROWS = 32
VMEM_LIMIT_V7X = 56 * 1024 * 1024


def _params(*sem):
    return pltpu.CompilerParams(dimension_semantics=sem, vmem_limit_bytes=VMEM_LIMIT_V7X)


def _resident(shape, index_map):
    return pl.BlockSpec(shape, index_map, pipeline_mode=pl.Buffered(1))


def _sigmoid(x):
    return 1.0 / (1.0 + jnp.exp(-x))


def _dot(a, b):
    return jnp.dot(a, b, preferred_element_type=F32)


def _dot_nt(a, b):
    return lax.dot_general(a, b, (((1,), (1,)), ((), ())), preferred_element_type=F32)


def _modnorm(x, g, shift, scale):
    ms = jnp.mean(x * x, axis=-1, keepdims=True)
    y = x * lax.rsqrt(ms + NORM_EPS) * g
    return y * (1.0 + scale) + shift


def _ada_kernel(s_ref, w_ref, b_ref, o_ref):
    s = s_ref[...]
    s = s * _sigmoid(s)
    o_ref[...] = _dot(s.astype(BF16), w_ref[...].astype(BF16)) + b_ref[...]


def _ada_call(s_all, ada_w, ada_b):
    depth, d, n6 = ada_w.shape
    tn = 1024
    return pl.pallas_call(
        _ada_kernel,
        grid=(depth, n6 // tn),
        in_specs=[pl.BlockSpec((MOD_ROWS, d), lambda l, n: (0, 0)),
                  pl.BlockSpec((None, d, tn), lambda l, n: (l, 0, n)),
                  pl.BlockSpec((None, 1, tn), lambda l, n: (l, 0, n))],
        out_specs=pl.BlockSpec((None, MOD_ROWS, tn), lambda l, n: (l, 0, n)),
        out_shape=jax.ShapeDtypeStruct((depth, MOD_ROWS, n6), F32),
        compiler_params=_params("arbitrary", "arbitrary"),
        name="ada_mod",
    )(s_all, ada_w, ada_b.reshape(depth, 1, n6))


def _rope(a, cosf, sinf, first_half):
    fwd = pltpu.roll(a, HEAD_DIM - ROPE_FREQS, 1)
    bwd = pltpu.roll(a, ROPE_FREQS, 1)
    return a * cosf + jnp.where(first_half, fwd, bwd) * sinf


def _proj_kernel(x_ref, mod_ref, g_ref, w_ref, *rest, n_out, chunk, rope_heads):
    if rope_heads is not None:
        cos_ref, sin_ref, o_ref, h_scr = rest
    else:
        o_ref, h_scr = rest
    h_scr[...] = _modnorm(x_ref[...], g_ref[...], mod_ref[0:1, :], mod_ref[1:2, :]).astype(BF16)
    if rope_heads is not None:
        cosf, sinf = cos_ref[...], sin_ref[...]
        lane = lax.broadcasted_iota(jnp.int32, cosf.shape, 1)
        first_half = (lane % (2 * ROPE_FREQS)) < ROPE_FREQS
    for j in range(n_out // chunk):
        acc = _dot(h_scr[...], w_ref[:, j * chunk:(j + 1) * chunk])
        for hh in range(chunk // HEAD_DIM):
            head = j * (chunk // HEAD_DIM) + hh
            a = acc[:, hh * HEAD_DIM:(hh + 1) * HEAD_DIM]
            if rope_heads is not None and head in rope_heads:
                a = _rope(a, cosf, sinf, first_half)
            o_ref[:, head * HEAD_DIM:(head + 1) * HEAD_DIM] = a.astype(BF16)


def _proj_call(x, mod_l, g, w_all, layer_idx, is_ctx, rope=None, rope_heads=None):
    b_, s_, d = x.shape
    n_out = w_all.shape[2]
    tm = 256
    mod_row = (lambda b: b_) if is_ctx else (lambda b: b)
    in_specs = [pl.BlockSpec((None, tm, d), lambda b, i: (b, i, 0)),
                pl.BlockSpec((None, 6, d), lambda b, i: (mod_row(b), 0, 0)),
                pl.BlockSpec((1, d), lambda b, i: (0, 0)),
                _resident((None, d, n_out), lambda b, i: (layer_idx, 0, 0))]
    args = [x, mod_l, g, w_all]
    if rope is not None:
        in_specs += [pl.BlockSpec((tm, HEAD_DIM), lambda b, i: (i, 0)),
                     pl.BlockSpec((tm, HEAD_DIM), lambda b, i: (i, 0))]
        args += list(rope)
    else:
        rope_heads = None
    return pl.pallas_call(
        functools.partial(_proj_kernel, n_out=n_out, chunk=512, rope_heads=rope_heads),
        grid=(b_, s_ // tm),
        in_specs=in_specs,
        out_specs=pl.BlockSpec((None, tm, n_out), lambda b, i: (b, i, 0)),
        out_shape=jax.ShapeDtypeStruct((b_, s_, n_out), BF16),
        scratch_shapes=[pltpu.VMEM((tm, d), BF16)],
        compiler_params=_params("arbitrary", "arbitrary"),
        name="proj_ctx" if is_ctx else "proj_lat",
    )(*args)


def _outproj_kernel(ya_ref, yb_ref, x_ref, mod_ref, wa_ref, wb_ref, o_ref):
    acc = _dot(ya_ref[...], wa_ref[...]) + _dot(yb_ref[...], wb_ref[...])
    o_ref[...] = x_ref[...] + mod_ref[2:3, :] * acc


def _outproj_call(ya, yb, x, mod_l, w_all, layer_idx, is_ctx):
    b_, s_, d = x.shape
    tm = 256
    half = ya.shape[2]
    mod_row = (lambda b: b_) if is_ctx else (lambda b: b)
    return pl.pallas_call(
        _outproj_kernel,
        grid=(b_, s_ // tm),
        in_specs=[pl.BlockSpec((None, tm, half), lambda b, i: (b, i, 0)),
                  pl.BlockSpec((None, tm, half), lambda b, i: (b, i, 0)),
                  pl.BlockSpec((None, tm, d), lambda b, i: (b, i, 0)),
                  pl.BlockSpec((None, 6, d), lambda b, i: (mod_row(b), 0, 0)),
                  _resident((None, half, d), lambda b, i: (layer_idx, 0, 0)),
                  _resident((None, half, d), lambda b, i: (layer_idx, 1, 0))],
        out_specs=pl.BlockSpec((None, tm, d), lambda b, i: (b, i, 0)),
        out_shape=jax.ShapeDtypeStruct(x.shape, F32),
        compiler_params=_params("arbitrary", "arbitrary"),
        name="outproj_ctx" if is_ctx else "outproj_lat",
    )(ya, yb, x, mod_l, w_all, w_all)


FFN_HALO = 16
FFN_SUB = 256


def _ffn_kernel(x_ref, xp_ref, xn_ref, mod_ref, g_ref, fg_ref, wg_ref, wv_ref, cwg_ref, cwv_ref, cbg_ref, cbv_ref,
                wd_ref, o_ref, h_scr, acc_scr, ug_scr, uv_scr, y_scr, *, tm, final_norm):
    i = pl.program_id(1)
    f = pl.program_id(2)
    rows = tm + 2 * FFN_HALO

    @pl.when(f == 0)
    def _():
        g, shift, scale = g_ref[...], mod_ref[3:4, :], mod_ref[4:5, :]
        h_scr[FFN_HALO:FFN_HALO + tm, :] = _modnorm(x_ref[...], g, shift, scale).astype(BF16)
        hp = jnp.where(i > 0, _modnorm(xp_ref[...], g, shift, scale), 0.0)
        hn = jnp.where(i < pl.num_programs(1) - 1, _modnorm(xn_ref[...], g, shift, scale), 0.0)
        h_scr[0:FFN_HALO, :] = hp.astype(BF16)
        h_scr[FFN_HALO + tm:rows, :] = hn.astype(BF16)
        acc_scr[...] = jnp.zeros_like(acc_scr)

    h = h_scr[...]
    n_sub = wd_ref.shape[0] // FFN_SUB

    lanes = FFN_SUB // 128

    def up(s):
        cols = slice(s * FFN_SUB, (s + 1) * FFN_SUB)
        for u_scr, w_ref in ((ug_scr, wg_ref), (uv_scr, wv_ref)):
            res = _dot(h, w_ref[:, cols])
            for j in range(lanes):
                u_scr[s * lanes + j] = res[:, j * 128:(j + 1) * 128]

    def conv(u_scr, slab, cw_ref, cb_ref):
        cols = slice(slab * 128, (slab + 1) * 128)
        before, even, odd, after = (u_scr[slab, pl.ds(FFN_HALO + k, tm // 2, stride=2), :] for k in (-1, 0, 1, 2))
        w0, w1, w2, b = cw_ref[0:1, cols], cw_ref[1:2, cols], cw_ref[2:3, cols], cb_ref[:, cols]
        return jnp.concatenate([w0 * before + w1 * even + w2 * odd + b,
                                w0 * even + w1 * odd + w2 * after + b], axis=0)

    def down(s):
        zs = []
        for j in range(lanes):
            gate = conv(ug_scr, s * lanes + j, cwg_ref, cbg_ref)
            val = conv(uv_scr, s * lanes + j, cwv_ref, cbv_ref)
            zs.append((gate * _sigmoid(gate) * val).astype(BF16))
        acc_scr[...] += _dot(jnp.concatenate(zs, axis=1), wd_ref[s * FFN_SUB:(s + 1) * FFN_SUB, :])

    up(0)
    for s in range(1, n_sub):
        up(s)
        down(s - 1)
    down(n_sub - 1)

    @pl.when(f == pl.num_programs(2) - 1)
    def _():
        for j in range(acc_scr.shape[1] // 128):
            a = acc_scr[:, j * 128:(j + 1) * 128]
            y_scr[j, pl.ds(0, tm // 2, stride=2), :] = a[:tm // 2]
            y_scr[j, pl.ds(1, tm // 2, stride=2), :] = a[tm // 2:]
        ffn = jnp.concatenate([y_scr[j] for j in range(y_scr.shape[0])], axis=1)
        y = x_ref[...] + mod_ref[5:6, :] * ffn
        if final_norm:
            ms = jnp.mean(y * y, axis=-1, keepdims=True)
            y = y * lax.rsqrt(ms + NORM_EPS) * fg_ref[...]
        o_ref[...] = y


def _ffn_call(x, mod_l, g, w_up, dw_w, dw_b, w_down, layer_idx, is_ctx, final_g=None):
    b_, s_, d = x.shape
    hid = w_down.shape[1]
    tm = min(512, s_)
    tf = 512
    n_f = hid // tf
    nh = tm // FFN_HALO
    last_halo = s_ // FFN_HALO - 1
    mod_row = (lambda b: b_) if is_ctx else (lambda b: b)
    return pl.pallas_call(
        functools.partial(_ffn_kernel, tm=tm, final_norm=final_g is not None),
        grid=(b_, s_ // tm, n_f),
        in_specs=[pl.BlockSpec((None, tm, d), lambda b, i, f: (b, i, 0)),
                  pl.BlockSpec((None, FFN_HALO, d), lambda b, i, f: (b, jnp.maximum(i * nh - 1, 0), 0)),
                  pl.BlockSpec((None, FFN_HALO, d), lambda b, i, f: (b, jnp.minimum((i + 1) * nh, last_halo), 0)),
                  pl.BlockSpec((None, 6, d), lambda b, i, f: (mod_row(b), 0, 0)),
                  pl.BlockSpec((1, d), lambda b, i, f: (0, 0)),
                  pl.BlockSpec((1, d), lambda b, i, f: (0, 0)),
                  pl.BlockSpec((None, d, tf), lambda b, i, f: (layer_idx, 0, f)),
                  pl.BlockSpec((None, d, tf), lambda b, i, f: (layer_idx, 0, n_f + f)),
                  pl.BlockSpec((None, 3, tf), lambda b, i, f: (layer_idx, 0, f)),
                  pl.BlockSpec((None, 3, tf), lambda b, i, f: (layer_idx, 0, n_f + f)),
                  pl.BlockSpec((None, 1, tf), lambda b, i, f: (layer_idx, 0, f)),
                  pl.BlockSpec((None, 1, tf), lambda b, i, f: (layer_idx, 0, n_f + f)),
                  pl.BlockSpec((None, tf, d), lambda b, i, f: (layer_idx, f, 0))],
        out_specs=pl.BlockSpec((None, tm, d), lambda b, i, f: (b, i, 0)),
        out_shape=jax.ShapeDtypeStruct(x.shape, F32),
        scratch_shapes=[pltpu.VMEM((tm + 2 * FFN_HALO, d), BF16), pltpu.VMEM((tm, d), F32),
                        pltpu.VMEM((tf // 128, tm + 2 * FFN_HALO, 128), F32),
                        pltpu.VMEM((tf // 128, tm + 2 * FFN_HALO, 128), F32),
                        pltpu.VMEM((d // 128, tm, 128), F32)],
        compiler_params=_params("arbitrary", "arbitrary", "arbitrary"),
        name="ffn_ctx" if is_ctx else "ffn_lat",
    )(x, x, x, mod_l, g, g if final_g is None else final_g, w_up, w_up, dw_w, dw_w, dw_b, dw_b, w_down)


def _rows_of_step(t, part):
    return pl.ds(t * S5_TPITCH + part, S5_BLOCKS, stride=2)


def _rows_of_block(s, part):
    return pl.ds(2 * s + part, S5_CHUNK, stride=S5_TPITCH)


def _s5_kernel(uf_ref, ub_ref, wb_ref, wc_ref, lam_ref, yf_ref, yb_ref, *scratch):
    s = pl.program_id(1)
    u, y = (uf_ref, ub_ref), (yf_ref, yb_ref)
    scr, h_scr = scratch[:8], scratch[8]
    bu, st = (scr[0:2], scr[2:4]), (scr[4:6], scr[6:8])

    @pl.when((s == 0) & (pl.program_id(0) == 0))
    def _():
        for ref in scr + (h_scr,):
            ref[...] = jnp.zeros_like(ref)

    def run(drv):
        scn = 1 - drv

        def drive(r, c):
            res = _dot(u[r][:, c * 128:(c + 1) * 128], wb_ref[r, c])
            for sl in range(4):
                for part in range(2):
                    col = sl * 256 + part * 128
                    bu[r][drv][_rows_of_block(4 * c + sl, part), :] = res[:, col:col + 128]

        def step(carry, t, r):
            hr, hi = carry
            lr, li = lam_ref[2 * r], lam_ref[2 * r + 1]
            nr = lr * hr - li * hi + bu[r][scn][_rows_of_step(t, 0), :]
            ni = lr * hi + li * hr + bu[r][scn][_rows_of_step(t, 1), :]
            st[r][scn][_rows_of_step(t, 0), :] = nr
            st[r][scn][_rows_of_step(t, 1), :] = ni
            return nr, ni

        def readout(r, c):
            parts = [st[r][drv][_rows_of_block(4 * c + sl, part), :].astype(BF16)
                     for sl in range(4) for part in range(2)]
            y[r][:, c * 128:(c + 1) * 128] = _dot(jnp.concatenate(parts, axis=1), wc_ref[r, c])

        carry = [tuple(jnp.where(s == 1, 0.0, h_scr[2 * r + part]) for part in range(2)) for r in range(2)]
        per_slice = S5_CHUNK // 16
        for it in range(16):
            r, c = divmod(it, 8)
            drive(r, c)
            for t in range(it * per_slice, (it + 1) * per_slice):
                carry[0] = step(carry[0], t, 0)
                carry[1] = step(carry[1], S5_CHUNK - 1 - t, 1)
            readout(r, c)
        for r in range(2):
            for part in range(2):
                h_scr[2 * r + part] = carry[r][part]

    for drv in range(2):
        pl.when(s % 2 == drv)(functools.partial(run, drv))


def _s5_call(u_all, n_ctx_chunks, wb, wc, lam):
    b_, t_len, ch = u_all.shape
    tc = S5_CHUNK
    n = t_len // tc

    def fwd(p):
        return p

    def bwd(p):
        return jnp.where(p < n_ctx_chunks, n_ctx_chunks - 1 - p, n + n_ctx_chunks - 1 - p)

    def spec(order, lag):
        return pl.BlockSpec((None, tc, ch), lambda b, s: (b, order(jnp.clip(s - lag, 0, n - 1)), 0))

    return pl.pallas_call(
        _s5_kernel,
        grid=(b_, n + 2),
        in_specs=[spec(fwd, 0), spec(bwd, 0),
                  _resident(wb.shape, lambda b, s: (0, 0, 0, 0)),
                  _resident(wc.shape, lambda b, s: (0, 0, 0, 0)),
                  _resident(lam.shape, lambda b, s: (0, 0, 0))],
        out_specs=[spec(fwd, 2), spec(bwd, 2)],
        out_shape=[jax.ShapeDtypeStruct((b_, t_len, ch), F32)] * 2,
        scratch_shapes=[pltpu.VMEM((S5_CHUNK * S5_TPITCH, 128), F32)] * 8
                       + [pltpu.VMEM((4, S5_BLOCKS, 128), F32)],
        compiler_params=_params("arbitrary", "arbitrary"),
        name="s5_scan",
    )(u_all, u_all, wb, wc, lam)


def _s5_pack(lam_re, lam_im, log_dt, b_re, b_im, c_re, c_im):
    lam_re = jnp.minimum(lam_re.astype(F32), -1e-4)
    lam_im = lam_im.astype(F32)
    dt = jnp.exp(log_dt.astype(F32))[..., None]
    mag = jnp.exp(lam_re * dt)
    lb_re, lb_im = mag * jnp.cos(lam_im * dt), mag * jnp.sin(lam_im * dt)
    den = lam_re * lam_re + lam_im * lam_im
    num_re = lb_re - 1.0
    coef_re = (num_re * lam_re + lb_im * lam_im) / den
    coef_im = (lb_im * lam_re - num_re * lam_im) / den
    b_re, b_im = b_re.astype(F32), b_im.astype(F32)
    bb_re = coef_re[..., None] * b_re - coef_im[..., None] * b_im
    bb_im = coef_re[..., None] * b_im + coef_im[..., None] * b_re
    lam = jnp.stack([lb_re[0], lb_im[0], lb_re[1], lb_im[1]]).reshape(4, S5_BLOCKS, 128)
    eye4, eye2 = jnp.eye(4, dtype=F32), jnp.eye(2, dtype=F32)
    bb = jnp.stack([bb_re, bb_im], axis=1).reshape(2, 2, 8, 4, 2, S5_STATE, S5_GROUP_CH)
    wb = jnp.einsum('rxcsgph,ts,kg->rctkhsxgp', bb, eye4, eye2).reshape(2, 8, 128, 1024)
    cc = jnp.stack([c_re.astype(F32), -c_im.astype(F32)], axis=1).reshape(2, 2, 8, 4, 2, S5_GROUP_CH, S5_STATE)
    wc = jnp.einsum('rxcsghp,ts,kg->rcsxgptkh', cc, eye4, eye2).reshape(2, 8, 1024, 128)
    return wb.astype(BF16), wc.astype(BF16), lam


def _glu_kernel(yf_ref, yb_ref, u_ref, d_ref, w_ref, o_ref):
    y = yf_ref[...] + yb_ref[...] + d_ref[...] * u_ref[...].astype(F32)
    t = 0.5 * y * (1.0 + jnp.tanh(math.sqrt(2.0 / math.pi) * (y + 0.044715 * (y * y * y))))
    gate = _dot(t.astype(BF16), w_ref[...])
    o_ref[...] = (t * _sigmoid(gate)).astype(BF16)


def _glu_call(yf, yb, z, d_skip, w_all, layer_idx):
    b_, s_, ch = yf.shape
    tm = 256
    return pl.pallas_call(
        _glu_kernel,
        grid=(b_, s_ // tm),
        in_specs=[pl.BlockSpec((None, tm, ch), lambda b, i: (b, i, 0)),
                  pl.BlockSpec((None, tm, ch), lambda b, i: (b, i, 0)),
                  pl.BlockSpec((None, tm, ch), lambda b, i: (b, i, 0)),
                  pl.BlockSpec((1, ch), lambda b, i: (0, 0)),
                  _resident((None, ch, ch), lambda b, i: (layer_idx, 0, 0))],
        out_specs=pl.BlockSpec((None, tm, ch), lambda b, i: (b, i, 0)),
        out_shape=jax.ShapeDtypeStruct((b_, s_, ch), BF16),
        compiler_params=_params("arbitrary", "arbitrary"),
        name="s5_glu",
    )(yf, yb, z, d_skip, w_all)


def _logits(q, segs):
    logits = []
    for k, _, bias, mask in segs:
        s = _dot_nt(q, k) * LOGIT_SCALE
        if bias is not None:
            s = s + bias
        if mask is not None:
            s = jnp.where(mask, s, NEG_INF)
        logits.append(s)
    return logits


LOGIT_SCALE = HEAD_DIM ** -0.5 * LOG2_E


def _softmax_pv(logits, values, sink=None, raw=False):
    m = _row_max(logits, sink)
    return _normalise(*_exp_pv(logits, values, m, raw), m, sink)


def _row_max(logits, sink):
    m = jnp.max(functools.reduce(jnp.maximum, [_fold_lanes(s(), jnp.maximum) for s in logits]),
                axis=-1, keepdims=True)
    return m if sink is None else jnp.maximum(m, sink)


def _exp_pv(logits, values, m, raw):
    psum = None
    out = 0.0
    for s, v in zip(logits, values):
        p = jnp.exp2((s() - m) * LOGIT_SCALE) if raw else jnp.exp2(s() - m)
        folded = _fold_lanes(p, jnp.add)
        psum = folded if psum is None else psum + folded
        out = out + _dot(p.astype(BF16), v())
    return out, psum


def _normalise(out, psum, m, sink):
    denom = jnp.sum(psum, axis=-1, keepdims=True)
    if sink is not None:
        denom = denom + jnp.exp2(sink - m)
    return out / denom


def _attend(q, segs, sink=None):
    return _softmax_pv([functools.partial(lambda s: s, s) for s in _logits(q, segs)],
                       [functools.partial(lambda v: v, v) for _, v, _, _ in segs], sink)


def _attend_pipelined(units, s_scr):
    slots = s_scr.shape[0]

    def stage(u):
        q, segs, sink, emit = units[u]()
        raw = sink is None and all(bias is None and mask is None for _, _, bias, mask in segs)
        col = 0
        for s in ([_dot_nt(q, k) for k, _, _, _ in segs] if raw else _logits(q, segs)):
            s_scr[u % slots, :, col:col + s.shape[1]] = s
            col += s.shape[1]
        logits, values, col = [], [], 0
        for k, v, _, _ in segs:
            for j in range(0, k.shape[0], ATT_PIECE):
                hi = min(j + ATT_PIECE, k.shape[0])
                logits.append(functools.partial(lambda slot, a, b: s_scr[slot, :, a:b], u % slots, col + j, col + hi))
                values.append(functools.partial(_rows, v, slice(j, hi)))
            col += k.shape[0]
        return logits, values, sink, emit, raw

    pending = stage(0)
    for u in range(len(units)):
        logits, values, sink, emit, raw = pending
        if u + 1 < len(units):
            pending = stage(u + 1)
        emit(_softmax_pv(logits, values, sink, raw))


ATT_PIECE = 256


def _rows(v, rows):
    return v(rows) if callable(v) else v[rows]


def _fold_lanes(x, op):
    return functools.reduce(op, [x[:, i:i + 128] for i in range(0, x.shape[1], 128)])


def _head(ref, h, width=HEAD_DIM):
    return ref[:, h * width:(h + 1) * width]


def _stack_heads(q_ref, kh, group):
    return jnp.concatenate([_head(q_ref, kh * group + g) for g in range(group)], axis=0)


def _sink_column(sink_ref, kh, group, rows):
    return jnp.concatenate([jnp.full((rows, 1), sink_ref[kh * group + g] * LOG2_E, F32) for g in range(group)],
                           axis=0)


def _unstack_heads(o_ref, o, kh, group, rows):
    for g in range(group):
        h = kh * group + g
        o_ref[:, h * HEAD_DIM:(h + 1) * HEAD_DIM] = o[g * rows:(g + 1) * rows].astype(o_ref.dtype)


def _dense_attn_kernel(q_ref, k_ref, v_ref, *rest, kv_heads, group, has_sink):
    if has_sink:
        sink_ref, o_ref = rest
    else:
        (o_ref,) = rest
    rows = q_ref.shape[0]
    for kh in range(kv_heads):
        q = _stack_heads(q_ref, kh, group)
        sink = _sink_column(sink_ref, kh, group, rows) if has_sink else None
        o = _attend(q, [(_head(k_ref, kh), _head(v_ref, kh), None, None)], sink)
        _unstack_heads(o_ref, o, kh, group, rows)


def _dense_ctx_attn_call(z_ctx, q_col, k_col, v_col, kv_heads, group, sink=None):
    b_, c_len, _ = z_ctx.shape
    qw, kw = kv_heads * group * HEAD_DIM, kv_heads * HEAD_DIM
    in_specs = [pl.BlockSpec((None, c_len, qw), lambda b: (b, 0, q_col)),
                pl.BlockSpec((None, c_len, kw), lambda b: (b, 0, k_col)),
                pl.BlockSpec((None, c_len, kw), lambda b: (b, 0, v_col))]
    args = [z_ctx, z_ctx, z_ctx]
    if sink is not None:
        in_specs.append(pl.BlockSpec(memory_space=pltpu.SMEM))
        args.append(sink)
    return pl.pallas_call(
        functools.partial(_dense_attn_kernel, kv_heads=kv_heads, group=group, has_sink=sink is not None),
        grid=(b_,),
        in_specs=in_specs,
        out_specs=pl.BlockSpec((None, c_len, qw), lambda b: (b, 0, 0)),
        out_shape=jax.ShapeDtypeStruct((b_, c_len, qw), BF16),
        compiler_params=_params("arbitrary"),
        name="ctx_attn",
    )(*args)


def _na_kernel(q_ref, k0_ref, k1_ref, k2_ref, v0_ref, v1_ref, v2_ref, kc_ref, vc_ref, bias_ref, o_ref, s_scr):
    kb = k0_ref.shape[0]

    def unit(h):
        def emit(o):
            o_ref[:, h * HEAD_DIM:(h + 1) * HEAD_DIM] = o.astype(o_ref.dtype)

        segs = [(_head(k_ref, h), _head(v_ref, h), bias_ref[h, :, t * kb:(t + 1) * kb], None)
                for t, (k_ref, v_ref) in enumerate(((k0_ref, v0_ref), (k1_ref, v1_ref), (k2_ref, v2_ref)))]
        segs.append((_head(kc_ref, h), _head(vc_ref, h), None, None))
        return _head(q_ref, h), segs, None, emit

    _attend_pipelined([functools.partial(unit, h) for h in range(HALF_MIX // HEAD_DIM)], s_scr)


def _na_bias_tables(rpb, rows):
    tables = []
    a = np.arange(NA_QROWS)[:, None]
    c = np.arange(NA_KROWS)[None, :]
    qc = np.arange(GRID_W)[:, None]
    kc = np.arange(GRID_W)[None, :]
    ws = np.clip(qc - NA_COLS // 2, 0, GRID_W - NA_COLS)
    col_ok = (kc >= ws) & (kc < ws + NA_COLS)
    dc = np.clip(kc - qc + NA_COLS - 1, 0, 2 * NA_COLS - 2)
    onehot = (np.arange(2 * NA_COLS - 1)[:, None, None] == dc[None]).astype(np.float32)
    by_col = jnp.einsum('hrj,jqk->hrqk', rpb.astype(F32), onehot, precision=lax.Precision.HIGHEST)
    for r0 in (0, NA_QROWS, rows - NA_QROWS):
        base = int(np.clip(r0 - NA_ROWS // 2, 0, rows - NA_KROWS))
        qrow, krow = r0 + a, base + c
        rs = np.clip(qrow - NA_ROWS // 2, 0, rows - NA_ROWS)
        row_ok = (krow >= rs) & (krow < rs + NA_ROWS)
        dr = np.clip(krow - qrow + NA_ROWS - 1, 0, 2 * NA_ROWS - 2)
        vals = by_col[:, dr]
        ok = row_ok[:, :, None, None] & col_ok[None, None, :, :]
        t = jnp.where(ok[None], vals * LOG2_E, NEG_INF).transpose(0, 1, 3, 2, 4)
        tables.append(t.reshape(rpb.shape[0], NA_QROWS * GRID_W, NA_KROWS * GRID_W))
    return jnp.stack(tables)


def _na_call(z_lat, z_ctx, bias):
    b_, l_len, _ = z_lat.shape
    c_len = z_ctx.shape[1]
    qb = NA_QROWS * GRID_W
    nblk = l_len // qb
    w = HALF_MIX

    def kmap(t, col):
        return lambda b, i: (b, jnp.clip(i - 1, 0, nblk - 3) + t, col)

    def variant(b, i):
        return (jnp.where(i == 0, 0, jnp.where(i == nblk - 1, 2, 1)), 0, 0, 0)

    return pl.pallas_call(
        _na_kernel,
        grid=(b_, nblk),
        in_specs=[pl.BlockSpec((None, qb, w), lambda b, i: (b, i, 1))]
                 + [pl.BlockSpec((None, qb, w), kmap(t, 2)) for t in range(3)]
                 + [pl.BlockSpec((None, qb, w), kmap(t, 3)) for t in range(3)]
                 + [pl.BlockSpec((None, c_len, w), lambda b, i: (b, 0, 2)),
                    pl.BlockSpec((None, c_len, w), lambda b, i: (b, 0, 3)),
                    pl.BlockSpec((None,) + bias.shape[1:], variant)],
        out_specs=pl.BlockSpec((None, qb, w), lambda b, i: (b, i, 0)),
        out_shape=jax.ShapeDtypeStruct((b_, l_len, w), BF16),
        scratch_shapes=[pltpu.VMEM((2, qb, NA_KROWS * GRID_W + c_len), F32)],
        compiler_params=_params("arbitrary", "arbitrary"),
        name="na_attn",
    )(z_lat, z_lat, z_lat, z_lat, z_lat, z_lat, z_lat, z_ctx, z_ctx, bias)


def _diff_kernel(q_ref, kc_ref, vc_ref, *rest, has_lat, out_scale):
    if has_lat:
        kl_ref, vl_ref, lam_ref, g_ref, o_ref, s_scr = rest
    else:
        lam_ref, g_ref, o_ref, s_scr = rest
    lam = lam_ref[0]
    first = {}

    def unit(h, c):
        def emit(o):
            if c == 0:
                first[h] = o
                return
            o = first.pop(h) - lam * o
            ms = jnp.mean(o * o, axis=-1, keepdims=True)
            y = o * lax.rsqrt(ms + NORM_EPS) * g_ref[...] * out_scale
            o_ref[:, h * 2 * HEAD_DIM:(h + 1) * 2 * HEAD_DIM] = y.astype(o_ref.dtype)

        vcols = slice(h * 2 * HEAD_DIM, (h + 1) * 2 * HEAD_DIM)
        segs = [(_head(kc_ref, 2 * h + c), lambda rows: vc_ref[rows, vcols], None, None)]
        if has_lat:
            segs.append((_head(kl_ref, 2 * h + c), lambda rows: vl_ref[rows, vcols], None, None))
        return _head(q_ref, 2 * h + c), segs, None, emit

    _attend_pipelined([functools.partial(unit, h, c) for h in range(HALF_MIX // (2 * HEAD_DIM)) for c in range(2)],
                      s_scr)


def _diff_call(z_q, z_ctx, z_lat, lam, subln_g, out_scale):
    b_, s_, _ = z_q.shape
    c_len = z_ctx.shape[1]
    tq = 256
    w = HALF_MIX
    in_specs = [pl.BlockSpec((None, tq, w), lambda b, i: (b, i, 0)),
                pl.BlockSpec((None, c_len, w), lambda b, i: (b, 0, 1)),
                pl.BlockSpec((None, c_len, w), lambda b, i: (b, 0, 2))]
    args = [z_q, z_ctx, z_ctx]
    if z_lat is not None:
        l_len = z_lat.shape[1]
        in_specs += [pl.BlockSpec((None, l_len, w), lambda b, i: (b, 0, 1)),
                     pl.BlockSpec((None, l_len, w), lambda b, i: (b, 0, 2))]
        args += [z_lat, z_lat]
    in_specs += [pl.BlockSpec(memory_space=pltpu.SMEM), pl.BlockSpec((1, 2 * HEAD_DIM), lambda b, i: (0, 0))]
    args += [lam, subln_g]
    n_keys = c_len + (0 if z_lat is None else z_lat.shape[1])
    return pl.pallas_call(
        functools.partial(_diff_kernel, has_lat=z_lat is not None, out_scale=out_scale),
        grid=(b_, s_ // tq),
        in_specs=in_specs,
        out_specs=pl.BlockSpec((None, tq, w), lambda b, i: (b, i, 0)),
        out_shape=jax.ShapeDtypeStruct((b_, s_, w), BF16),
        scratch_shapes=[pltpu.VMEM((2, tq, n_keys), F32)],
        compiler_params=_params("arbitrary", "arbitrary"),
        name="diff_attn_lat" if z_lat is not None else "diff_attn_ctx",
    )(*args)


def _swa_kernel(q_ref, kp_ref, kn0_ref, kn_ref, vp_ref, vn0_ref, vn_ref, kc_ref, vc_ref, sink_ref, o_ref, s_scr,
                *, kv_heads, group, stack):
    n = pl.program_id(1)
    nb = pl.num_programs(1)
    blk = SWA_BLOCK
    qi = lax.broadcasted_iota(jnp.int32, (stack * blk, blk), 0) % blk
    kj = lax.broadcasted_iota(jnp.int32, (stack * blk, blk), 1)
    prev_ok = (kj >= qi) & (n > 0)
    next_ok = (kj <= qi) & (n < nb - 1)

    def unit(kh, part):
        first = kh * (group // stack) + part
        segs = [(_head(kp_ref, kh), _head(vp_ref, kh), None, prev_ok),
                (_head(kn0_ref, kh), _head(vn0_ref, kh), None, None),
                (_head(kn_ref, kh), _head(vn_ref, kh), None, next_ok),
                (_head(kc_ref, kh), _head(vc_ref, kh), None, None)]
        return (_stack_heads(q_ref, first, stack), segs, _sink_column(sink_ref, first, stack, blk),
                lambda o: _unstack_heads(o_ref, o, first, stack, blk))

    _attend_pipelined([functools.partial(unit, kh, part) for kh in range(kv_heads) for part in range(group // stack)],
                      s_scr)


def _swa_call(z_lat, z_ctx, sink, kv_heads=2, group=4):
    b_, l_len, _ = z_lat.shape
    c_len = z_ctx.shape[1]
    assert SWA_WINDOW == SWA_BLOCK
    blk = SWA_BLOCK
    stack = 2
    nb = l_len // blk
    qw, kw = kv_heads * group * HEAD_DIM, kv_heads * HEAD_DIM
    q_col = 3 * HALF_MIX // qw
    k_col = 4 * HALF_MIX // kw
    v_col = k_col + 1

    def prev(col): return lambda b, n: (b, jnp.maximum(n - 1, 0), col)
    def cur(col): return lambda b, n: (b, n, col)
    def nxt(col): return lambda b, n: (b, jnp.minimum(n + 1, nb - 1), col)

    return pl.pallas_call(
        functools.partial(_swa_kernel, kv_heads=kv_heads, group=group, stack=stack),
        grid=(b_, nb),
        in_specs=[pl.BlockSpec((None, blk, qw), lambda b, n: (b, n, q_col))]
                 + [pl.BlockSpec((None, blk, kw), m(k_col)) for m in (prev, cur, nxt)]
                 + [pl.BlockSpec((None, blk, kw), m(v_col)) for m in (prev, cur, nxt)]
                 + [pl.BlockSpec((None, c_len, kw), lambda b, n: (b, 0, k_col)),
                    pl.BlockSpec((None, c_len, kw), lambda b, n: (b, 0, v_col)),
                    pl.BlockSpec(memory_space=pltpu.SMEM)],
        out_specs=pl.BlockSpec((None, blk, qw), lambda b, n: (b, n, 0)),
        out_shape=jax.ShapeDtypeStruct((b_, l_len, qw), BF16),
        scratch_shapes=[pltpu.VMEM((2, stack * blk, 3 * blk + c_len), F32)],
        compiler_params=_params("arbitrary", "arbitrary"),
        name="swa_attn",
    )(z_lat, z_lat, z_lat, z_lat, z_lat, z_lat, z_lat, z_ctx, z_ctx, sink)


def _rope_tables(l_len):
    t = jnp.arange(l_len, dtype=jnp.int32)
    pos = jnp.stack([t // GRID_W, t % GRID_W], axis=-1).astype(F32)
    inv = ROPE_BASE ** (-2.0 * jnp.arange(ROPE_FREQS, dtype=F32) / (2 * ROPE_FREQS))
    ang = pos[:, :, None] * inv
    cos, sin = jnp.cos(ang), jnp.sin(ang)
    cosf = jnp.concatenate([cos[:, 0], cos[:, 0], cos[:, 1], cos[:, 1]], axis=-1)
    sinf = jnp.concatenate([-sin[:, 0], sin[:, 0], -sin[:, 1], sin[:, 1]], axis=-1)
    return cosf, sinf


_OD_ROPE_HEADS = frozenset(list(range(0, 16)) + list(range(24, 34)))


def kernel(x, c, ctx, c_ctx, ada_w, ada_b, norm_mix_g, norm_ffn_g, w_out, ffn_w_up, ffn_dw_w, ffn_dw_b,
           ffn_w_down, ev_w_in, s5_lam_re, s5_lam_im, s5_log_dt, s5_b_re, s5_b_im, s5_c_re, s5_c_im, s5_d,
           s5_w_glu, na_rpb, od_w_in, diff_lambda, diff_subln_g, swa_sink, final_norm_g):
    b_, l_len, d = x.shape
    depth = ada_w.shape[0]
    assert d == D_MODEL and b_ < MOD_ROWS
    rows = l_len // GRID_W

    w_out_h = w_out.astype(BF16)
    w_up_h = ffn_w_up.astype(BF16)
    w_down_h = ffn_w_down.astype(BF16)
    ev_w_h = ev_w_in.astype(BF16)
    od_w_h = od_w_in.astype(BF16)
    glu_w_h = s5_w_glu.astype(BF16)
    dw_b = ffn_dw_b.reshape(depth, 1, -1)

    s_all = jnp.zeros((MOD_ROWS, d), F32).at[:b_].set(c).at[b_].set(c_ctx)
    mod = _ada_call(s_all, ada_w, ada_b).reshape(depth, MOD_ROWS, 6, d)
    rope = _rope_tables(l_len)

    xc = ctx
    for l in range(depth):
        with_ctx = l < depth - 1
        i = l // 2
        mod_l = mod[l]
        g_mix = norm_mix_g[l].reshape(1, d)
        g_ffn = norm_ffn_g[l].reshape(1, d)
        if l % 2 == 0:
            z_lat = _proj_call(x, mod_l, g_mix, ev_w_h, i, False)
            z_ctx = _proj_call(xc, mod_l, g_mix, ev_w_h, i, True)
            wb, wc, lam = _s5_pack(s5_lam_re[i], s5_lam_im[i], s5_log_dt[i], s5_b_re[i], s5_b_im[i],
                                   s5_c_re[i], s5_c_im[i])
            c_len = z_ctx.shape[1]
            u_all = jnp.concatenate([z_ctx[:, :, :HALF_MIX], z_lat[:, :, :HALF_MIX]], axis=1)
            yf, yb_scan = _s5_call(u_all, c_len // S5_CHUNK, wb, wc, lam)
            ya_all = _glu_call(yf, yb_scan, u_all, s5_d[i].reshape(1, HALF_MIX), glu_w_h, i)
            ya = ya_all[:, c_len:]
            yb = _na_call(z_lat, z_ctx, _na_bias_tables(na_rpb[i], rows))
            if with_ctx:
                yac = ya_all[:, :c_len]
                ybc = _dense_ctx_attn_call(z_ctx, 1, 2, 3, HALF_MIX // HEAD_DIM, 1)
        else:
            lam_init = 0.8 - 0.6 * math.exp(-0.3 * l)
            z_lat = _proj_call(x, mod_l, g_mix, od_w_h, i, False, rope, _OD_ROPE_HEADS)
            z_ctx = _proj_call(xc, mod_l, g_mix, od_w_h, i, True)
            lf = diff_lambda[i].astype(F32)
            lam = (jnp.exp(jnp.sum(lf[0] * lf[1])) - jnp.exp(jnp.sum(lf[2] * lf[3])) + lam_init).reshape(1)
            subln = diff_subln_g[i].reshape(1, 2 * HEAD_DIM)
            sink = swa_sink[i].astype(F32)
            ya = _diff_call(z_lat, z_ctx, z_lat, lam, subln, 1.0 - lam_init)
            yb = _swa_call(z_lat, z_ctx, sink)
            if with_ctx:
                yac = _diff_call(z_ctx, z_ctx, None, lam, subln, 1.0 - lam_init)
                ybc = _dense_ctx_attn_call(z_ctx, 3, 16, 17, 2, 4, sink)
        x = _outproj_call(ya, yb, x, mod_l, w_out_h, l, False)
        x = _ffn_call(x, mod_l, g_ffn, w_up_h, ffn_dw_w, dw_b, w_down_h, l, False,
                      final_g=None if with_ctx else final_norm_g.reshape(1, d))
        if with_ctx:
            xc = _outproj_call(yac, ybc, xc, mod_l, w_out_h, l, True)
            xc = _ffn_call(xc, mod_l, g_ffn, w_up_h, ffn_dw_w, dw_b, w_down_h, l, True)
    return x
```
